```python
import jax, jax.numpy as jnp
from jax import lax
import numpy as np

D_MODEL = 2048
BATCH = 2
SEQ = 8192
DEPTH = 1

PLE_DIM = 256
A_HEADS = 16
A_KV_GROUPS = 4
A_REP = A_HEADS // A_KV_GROUPS
HEAD_DIM = 64
A_WIDTH = A_HEADS * HEAD_DIM
A_KV_WIDTH = A_KV_GROUPS * HEAD_DIM
CMP_LEN = 32
CMP_STRIDE = 16
CMP_RATIO = CMP_LEN // CMP_STRIDE
SEL_BLOCK = 64
SEL_TOP_N = 16
WINDOW = 512
Q_BLOCK = 128
FORCE_BONUS = 1000.0
B_WIDTH = D_MODEL // 2
B_GROUPS = 8
B_GROUP_DIM = B_WIDTH // B_GROUPS
B_CHUNK = 128
N_BRANCHES = 2
NEG = -1e30
EPS = 1e-6

IN_SPLITS = [A_WIDTH,
             6 * A_KV_WIDTH,
             3 * A_HEADS,
             A_WIDTH,
             2 * B_WIDTH,
             B_WIDTH,
             N_BRANCHES * D_MODEL]
IN_WIDTH = int(sum(IN_SPLITS))
SPLIT_IDX = [int(c) for c in np.cumsum(IN_SPLITS)[:-1]]

kernel_name = "hybrid_nsa_gmlp_gated_block"


def rms_norm(x, g):
    xf = x.astype(jnp.float32)
    y = xf * lax.rsqrt(jnp.mean(xf * xf, axis=-1, keepdims=True) + EPS)
    return (y * g.astype(jnp.float32)).astype(x.dtype)


def layer_norm(x, g, b):
    xf = x.astype(jnp.float32)
    mu = jnp.mean(xf, axis=-1, keepdims=True)
    xc = xf - mu
    y = xc * lax.rsqrt(jnp.mean(xc * xc, axis=-1, keepdims=True) + EPS)
    return (y * g.astype(jnp.float32) + b.astype(jnp.float32)).astype(x.dtype)


def alibi_slopes():
    h = np.arange(1, A_HEADS + 1, dtype=np.float32)
    s = np.power(np.float32(2.0), -8.0 * h / A_HEADS).astype(np.float32)
    return jnp.asarray(s.reshape(A_KV_GROUPS, A_REP))


def compress(kv, pe, w1, w2):
    b, s, g, d = kv.shape
    ch = kv.reshape(b, s // CMP_STRIDE, CMP_STRIDE, g, d)
    nc = s // CMP_STRIDE - CMP_RATIO + 1
    blocks = jnp.concatenate([ch[:, r:nc + r] for r in range(CMP_RATIO)], axis=2)
    blocks = blocks + pe[None, None, :, None, :]
    hid = jax.nn.silu(jnp.einsum('bnlgd,lde->bnge', blocks, w1))
    return jnp.einsum('bnge,ef->bngf', hid, w2)


def nsa_attention(q, k_c, v_c, k_s, v_s, k_w, v_w, gate_logits,
                  pe_k, w1_k, w2_k, pe_v, w1_v, w2_v):
    b, s = q.shape[:2]
    q = q.reshape(b, s, A_KV_GROUPS, A_REP, HEAD_DIM)
    kc = compress(k_c, pe_k, w1_k, w2_k)
    vc = compress(v_c, pe_v, w1_v, w2_v)
    nc = kc.shape[1]
    c_end = jnp.arange(nc, dtype=jnp.int32) * CMP_STRIDE + (CMP_LEN - 1)
    ns = s // SEL_BLOCK
    n_top = min(SEL_TOP_N, ns)
    ci = np.arange(nc)[:, None]
    sj = np.arange(ns)[None, :]
    overlap = jnp.asarray(((CMP_STRIDE * ci < SEL_BLOCK * sj + SEL_BLOCK) &
                           (CMP_STRIDE * ci + CMP_LEN > SEL_BLOCK * sj)).astype(np.float32))
    kb = k_s.reshape(b, ns, SEL_BLOCK, A_KV_GROUPS, HEAD_DIM).transpose(0, 3, 1, 2, 4)
    vb = v_s.reshape(b, ns, SEL_BLOCK, A_KV_GROUPS, HEAD_DIM).transpose(0, 3, 1, 2, 4)
    kw = jnp.pad(k_w, ((0, 0), (WINDOW, 0), (0, 0), (0, 0)))
    vw = jnp.pad(v_w, ((0, 0), (WINDOW, 0), (0, 0), (0, 0)))
    slopes = alibi_slopes()
    gates = jax.nn.sigmoid(gate_logits.astype(jnp.float32)).reshape(b, s, 3, A_KV_GROUPS, A_REP)
    scale = HEAD_DIM ** -0.5
    gather = jax.vmap(jax.vmap(lambda blk, ix: blk[ix]))
    sel_off = jnp.arange(SEL_BLOCK, dtype=jnp.int32)
    win_off = jnp.arange(WINDOW + Q_BLOCK, dtype=jnp.int32)
    blk_ids = jnp.arange(ns, dtype=jnp.int32)

    def block_fn(i):
        q0 = i * Q_BLOCK
        qb = lax.dynamic_slice_in_dim(q, q0, Q_BLOCK, axis=1)
        gb = lax.dynamic_slice_in_dim(gates, q0, Q_BLOCK, axis=1)
        t = q0 + jnp.arange(Q_BLOCK, dtype=jnp.int32)
        dc = t[:, None] - c_end[None, :]
        vc_mask = dc >= 0
        sc = jnp.einsum('bqgrd,bngd->bgrqn', qb, kc).astype(jnp.float32) * scale
        sc = jnp.where(vc_mask, sc - slopes[:, :, None, None] * dc.astype(jnp.float32), NEG)
        p_c = jax.nn.softmax(sc, axis=-1) * vc_mask
        o_c = jnp.einsum('bgrqn,bngd->bqgrd', p_c.astype(vc.dtype), vc)
        imp = jnp.einsum('bgrqn,nj->bgqj', p_c, overlap)
        cur = t // SEL_BLOCK
        forced = ((blk_ids[None, :] == 0) | (blk_ids[None, :] == cur[:, None]) |
                  (blk_ids[None, :] == cur[:, None] - 1))
        causal = blk_ids[None, :] * SEL_BLOCK <= t[:, None]
        score = jnp.where(causal, imp + jnp.where(forced, FORCE_BONUS, 0.0), -1.0)
        _, idx = lax.top_k(score, n_top)
        ks = gather(kb, idx)
        vs = gather(vb, idx)
        pos = idx[..., None] * SEL_BLOCK + sel_off
        ds = (t[None, None, :, None, None] - pos)[:, :, None]
        ss = jnp.einsum('bqgrd,bgqnkd->bgrqnk', qb, ks).astype(jnp.float32) * scale
        ss = jnp.where(ds >= 0, ss - slopes[None, :, :, None, None, None] * ds.astype(jnp.float32), NEG)
        p_s = jax.nn.softmax(ss.reshape(b, A_KV_GROUPS, A_REP, Q_BLOCK, n_top * SEL_BLOCK), axis=-1)
        p_s = p_s.reshape(b, A_KV_GROUPS, A_REP, Q_BLOCK, n_top, SEL_BLOCK)
        o_s = jnp.einsum('bgrqnk,bgqnkd->bqgrd', p_s.astype(vs.dtype), vs)
        kwb = lax.dynamic_slice_in_dim(kw, q0, WINDOW + Q_BLOCK, axis=1)
        vwb = lax.dynamic_slice_in_dim(vw, q0, WINDOW + Q_BLOCK, axis=1)
        kp = q0 - WINDOW + win_off
        dw = t[:, None] - kp[None, :]
        vw_mask = (dw >= 0) & (dw < WINDOW) & (kp[None, :] >= 0)
        sw = jnp.einsum('bqgrd,bkgd->bgrqk', qb, kwb).astype(jnp.float32) * scale
        sw = jnp.where(vw_mask, sw - slopes[:, :, None, None] * dw.astype(jnp.float32), NEG)
        p_w = jax.nn.softmax(sw, axis=-1)
        o_w = jnp.einsum('bgrqk,bkgd->bqgrd', p_w.astype(vwb.dtype), vwb)
        o = (gb[:, :, 0, :, :, None] * o_c + gb[:, :, 1, :, :, None] * o_s +
             gb[:, :, 2, :, :, None] * o_w)
        return o.astype(q.dtype)

    out = lax.map(block_fn, jnp.arange(s // Q_BLOCK, dtype=jnp.int32))
    return out.transpose(1, 0, 2, 3, 4, 5).reshape(b, s, A_WIDTH)


def spatial_gating(uv, ln_g, ln_b, w_s, b_s):
    uv = jax.nn.gelu(uv)
    u, v = jnp.split(uv, 2, axis=-1)
    v = layer_norm(v, ln_g, ln_b)
    b, s, _ = v.shape
    v = v.reshape(b, s // B_CHUNK, B_CHUNK, B_GROUPS, B_GROUP_DIM)
    tril = jnp.tril(jnp.ones((B_CHUNK, B_CHUNK), dtype=w_s.dtype))
    sv = jnp.einsum('gts,bcsgd->bctgd', w_s * tril, v) + b_s.T[None, None, :, :, None]
    return u * sv.reshape(b, s, B_WIDTH)


def setup_inputs(seed: int = 0) -> dict:
    key = jax.random.key(seed)
    ks = jax.random.split(key, 24)
    f32 = jnp.float32
    nrm = lambda k, shape, sc: jax.random.normal(k, shape, f32) * sc
    L = DEPTH
    return {
        "x": nrm(ks[0], (BATCH, SEQ, D_MODEL), 1.0),
        "p": nrm(ks[1], (DEPTH, BATCH, SEQ, PLE_DIM), 1.0),
        "norm_g": 1.0 + nrm(ks[2], (L, D_MODEL), 0.05),
        "w_in": nrm(ks[3], (L, D_MODEL, IN_WIDTH), D_MODEL ** -0.5),
        "cmp_pe_k": nrm(ks[4], (L, CMP_LEN, HEAD_DIM), 0.1),
        "cmp_w1_k": nrm(ks[5], (L, CMP_LEN, HEAD_DIM, HEAD_DIM), (CMP_LEN * HEAD_DIM) ** -0.5),
        "cmp_w2_k": nrm(ks[6], (L, HEAD_DIM, HEAD_DIM), HEAD_DIM ** -0.5),
        "cmp_pe_v": nrm(ks[7], (L, CMP_LEN, HEAD_DIM), 0.1),
        "cmp_w1_v": nrm(ks[8], (L, CMP_LEN, HEAD_DIM, HEAD_DIM), (CMP_LEN * HEAD_DIM) ** -0.5),
        "cmp_w2_v": nrm(ks[9], (L, HEAD_DIM, HEAD_DIM), HEAD_DIM ** -0.5),
        "ln_v_g": 1.0 + nrm(ks[10], (L, B_WIDTH), 0.05),
        "ln_v_b": nrm(ks[11], (L, B_WIDTH), 0.02),
        "sgu_w": nrm(ks[12], (L, B_GROUPS, B_CHUNK, B_CHUNK), B_CHUNK ** -0.5),
        "sgu_b": 1.0 + nrm(ks[13], (L, B_GROUPS, B_CHUNK), 0.05),
        "w_up_a": nrm(ks[14], (L, A_WIDTH, D_MODEL), A_WIDTH ** -0.5),
        "w_up_b": nrm(ks[15], (L, B_WIDTH, D_MODEL), B_WIDTH ** -0.5),
        "w_out": nrm(ks[16], (L, D_MODEL, D_MODEL), D_MODEL ** -0.5),
        "w_ple": nrm(ks[17], (L, PLE_DIM, D_MODEL), PLE_DIM ** -0.5),
        "w_ple_gate": nrm(ks[18], (L, D_MODEL, D_MODEL), D_MODEL ** -0.5),
        "final_g": 1.0 + nrm(ks[19], (D_MODEL,), 0.05),
    }


def reference(x, p, norm_g, w_in, cmp_pe_k, cmp_w1_k, cmp_w2_k, cmp_pe_v, cmp_w1_v, cmp_w2_v,
              ln_v_g, ln_v_b, sgu_w, sgu_b, w_up_a, w_up_b, w_out, w_ple, w_ple_gate, final_g):
    b, s, _ = x.shape
    for l in range(DEPTH):
        h = rms_norm(x, norm_g[l])
        proj = h @ w_in[l]
        q, kv, g_logits, z_a, uv, z_b, merge_logits = jnp.split(proj, SPLIT_IDX, axis=-1)
        k_c, v_c, k_s, v_s, k_w, v_w = [t.reshape(b, s, A_KV_GROUPS, HEAD_DIM)
                                        for t in jnp.split(kv, 6, axis=-1)]
        o_a = nsa_attention(q, k_c, v_c, k_s, v_s, k_w, v_w, g_logits,
                            cmp_pe_k[l], cmp_w1_k[l], cmp_w2_k[l],
                            cmp_pe_v[l], cmp_w1_v[l], cmp_w2_v[l]) * jax.nn.silu(z_a)
        o_b = spatial_gating(uv, ln_v_g[l], ln_v_b[l], sgu_w[l], sgu_b[l]) * jax.nn.silu(z_b)
        g_a, g_b = jnp.split(jax.nn.sigmoid(merge_logits), 2, axis=-1)
        merged = g_a * (o_a @ w_up_a[l]) + g_b * (o_b @ w_up_b[l])
        x = x + merged @ w_out[l]
        x = x + jax.nn.sigmoid(x @ w_ple_gate[l]) * (p[l] @ w_ple[l])
    return rms_norm(x, final_g)
```

```python
import functools

import numpy as np
import jax
import jax.numpy as jnp
from jax import lax
from jax.experimental import pallas as pl
from jax.experimental.pallas import tpu as pltpu

A_HEADS = 16
A_KV_GROUPS = 4
A_REP = A_HEADS // A_KV_GROUPS
HEAD_DIM = 64
A_WIDTH = A_HEADS * HEAD_DIM
A_KV_WIDTH = A_KV_GROUPS * HEAD_DIM
CMP_LEN = 32
CMP_STRIDE = 16
SEL_BLOCK = 64
SEL_TOP_N = 16
WINDOW = 512
Q_BLOCK = 128
FORCE_BONUS = 1000.0
B_GROUPS = 8
B_CHUNK = 128
NEG = -1e30
EPS = 1e-6

LANES = 128
QROWS = A_REP * Q_BLOCK
SEL_TILE = 512
WIN_KEYS = WINDOW + Q_BLOCK
ROW_TILE = 512
VMEM_LIMIT = 56 * 1024 * 1024

F32 = jnp.float32
BF16 = jnp.bfloat16


def _cparams(n_axes):
    return pltpu.CompilerParams(dimension_semantics=("arbitrary",) * n_axes,
                                vmem_limit_bytes=VMEM_LIMIT)


def _sigmoid(x):
    return 1.0 / (1.0 + jnp.exp(-x))


def _dot_nt(a, b):
    return lax.dot_general(a, b, (((1,), (1,)), ((), ())), preferred_element_type=F32)


def _norm_body(x_ref, g_ref, h_ref):
    x = x_ref[...]
    y = x * lax.rsqrt(jnp.mean(x * x, axis=-1, keepdims=True) + EPS)
    h_ref[...] = (y * g_ref[...]).astype(h_ref.dtype)


def _rmsnorm(x2d, g):
    m, d = x2d.shape
    return pl.pallas_call(
        _norm_body,
        grid=(m // ROW_TILE,),
        in_specs=[pl.BlockSpec((ROW_TILE, d), lambda i: (i, 0)),
                  pl.BlockSpec((1, d), lambda i: (0, 0))],
        out_specs=pl.BlockSpec((ROW_TILE, d), lambda i: (i, 0)),
        out_shape=jax.ShapeDtypeStruct((m, d), BF16),
        compiler_params=_cparams(1),
    )(x2d, g.reshape(1, d))


def _mm_body(h_ref, w_ref, o_ref):
    o_ref[...] = jnp.dot(h_ref[...], w_ref[...], preferred_element_type=F32).astype(o_ref.dtype)


def _project(h, w, out_dtype):
    m, k = h.shape
    n = w.shape[1]
    tn = min(n, 2048)
    return pl.pallas_call(
        _mm_body,
        grid=(n // tn, m // ROW_TILE),
        in_specs=[pl.BlockSpec((ROW_TILE, k), lambda j, i: (i, 0)),
                  pl.BlockSpec((k, tn), lambda j, i: (0, j))],
        out_specs=pl.BlockSpec((ROW_TILE, tn), lambda j, i: (i, j)),
        out_shape=jax.ShapeDtypeStruct((m, n), out_dtype),
        compiler_params=_cparams(2),
    )(h, w)


def _compress_one(c, pe, w1, w2p, ncp):
    half = CMP_STRIDE * HEAD_DIM
    hp = lax.Precision.HIGHEST
    h_lo = jnp.dot(c + pe[:, :half], w1[:half, :], preferred_element_type=F32, precision=hp)
    h_hi = jnp.dot(c + pe[:, half:], w1[half:, :], preferred_element_type=F32, precision=hp)
    pre = h_lo + pltpu.roll(h_hi, ncp - 1, 0)
    hid = pre * _sigmoid(pre)
    out = jnp.dot(hid, w2p, preferred_element_type=F32, precision=hp)
    row = lax.broadcasted_iota(jnp.int32, out.shape, 0)
    return jnp.where(row < ncp - 1, out, 0.0)


def _compress_body(ck_ref, cv_ref, pek_ref, w1k_ref, w2k_ref, pev_ref, w1v_ref, w2v_ref,
                   kc_ref, vc_ref):
    ncp = ck_ref.shape[0]
    kc = _compress_one(ck_ref[...], pek_ref[...], w1k_ref[...], w2k_ref[...], ncp)
    vc = _compress_one(cv_ref[...], pev_ref[...], w1v_ref[...], w2v_ref[...], ncp)
    kc_ref[...] = kc.astype(kc_ref.dtype)
    vc_ref[...] = vc.astype(vc_ref.dtype)


def _compress(ck, cv, pe_k, w1_k, w2_k, pe_v, w1_v, w2_v):
    b, g, ncp, cw = ck.shape
    flat = CMP_LEN * HEAD_DIM
    pad2 = lambda w: jnp.pad(w, ((0, 0), (0, LANES - HEAD_DIM)))
    tok = pl.BlockSpec((None, None, ncp, cw), lambda bi, gi: (bi, gi, 0, 0))
    full = lambda shp: pl.BlockSpec(shp, lambda bi, gi: (0,) * len(shp))
    out = pl.BlockSpec((None, None, ncp, LANES), lambda bi, gi: (bi, gi, 0, 0))
    return pl.pallas_call(
        _compress_body,
        grid=(b, g),
        in_specs=[tok, tok,
                  full((1, flat)), full((flat, HEAD_DIM)), full((HEAD_DIM, LANES)),
                  full((1, flat)), full((flat, HEAD_DIM)), full((HEAD_DIM, LANES))],
        out_specs=[out, out],
        out_shape=[jax.ShapeDtypeStruct((b, g, ncp, LANES), BF16)] * 2,
        compiler_params=_cparams(2),
    )(ck, cv,
      pe_k.reshape(1, flat), w1_k.reshape(flat, HEAD_DIM), pad2(w2_k),
      pe_v.reshape(1, flat), w1_v.reshape(flat, HEAD_DIM), pad2(w2_v))


def _nsa_body(q_ref, gl_ref, kc_ref, vc_ref, ks_ref, vs_ref, kw_ref, vw_ref,
              ab_ref, abc_ref, dm_ref, slope_ref, ovt_ref, o_ref, m_sc, acc_sc):
    i = pl.program_id(2)
    q0 = i * Q_BLOCK
    q = q_ref[...]
    ncp = kc_ref.shape[0]
    n_sel = ovt_ref.shape[0]

    rows = lax.broadcasted_iota(jnp.int32, (QROWS, ncp), 0)
    t_q = q0 + (rows & (Q_BLOCK - 1))
    c_end = lax.broadcasted_iota(jnp.int32, (QROWS, ncp), 1) * CMP_STRIDE + (CMP_LEN - 1)
    valid_c = t_q >= c_end
    sc = jnp.where(valid_c, _dot_nt(q, kc_ref[...]) + abc_ref[...], NEG)
    mc = jnp.max(sc, axis=1, keepdims=True)
    pc = jnp.where(valid_c, jnp.exp(sc - mc), 0.0)
    lc = jnp.sum(pc, axis=1, keepdims=True)
    pc = pc * jnp.where(lc > 0.0, 1.0 / lc, 0.0)
    o_c = jnp.dot(pc.astype(BF16), vc_ref[...], preferred_element_type=F32)[:, :HEAD_DIM]

    psum = pc[0:Q_BLOCK]
    for r in range(1, A_REP):
        psum = psum + pc[r * Q_BLOCK:(r + 1) * Q_BLOCK]
    p1 = psum.astype(BF16)
    r1 = psum - p1.astype(F32)
    p2 = r1.astype(BF16)
    p3 = (r1 - p2.astype(F32)).astype(BF16)
    ovt = ovt_ref[...]
    imp_t = _dot_nt(ovt, p3) + _dot_nt(ovt, p2) + _dot_nt(ovt, p1)

    blk = lax.broadcasted_iota(jnp.int32, (n_sel, Q_BLOCK), 0)
    t_l = q0 + lax.broadcasted_iota(jnp.int32, (n_sel, Q_BLOCK), 1)
    cur = t_l // SEL_BLOCK
    causal_b = blk * SEL_BLOCK <= t_l
    forced = (blk == 0) | (blk == cur) | (blk == cur - 1)
    score = jnp.where(causal_b, imp_t + jnp.where(forced, FORCE_BONUS, 0.0), -1.0)
    blk_f = blk.astype(F32)
    sel = jnp.zeros((n_sel, Q_BLOCK), F32)
    for _ in range(min(SEL_TOP_N, n_sel)):
        top = jnp.max(score, axis=0, keepdims=True)
        first = jnp.min(jnp.where(score == top, blk_f, float(n_sel)), axis=0, keepdims=True)
        pick = blk_f == first
        sel = jnp.where(pick, 1.0, sel)
        score = jnp.where(pick, -3e38, score)
    bias_t = jnp.where((sel > 0.0) & causal_b, 0.0, NEG)
    bias = bias_t.T.astype(BF16)
    q_aug = jnp.concatenate([jnp.concatenate([bias] * A_REP, axis=0), q], axis=1)

    slope = slope_ref[...]
    m_sc[...] = jnp.full(m_sc.shape, -3e38, F32)
    acc_sc[...] = jnp.zeros(acc_sc.shape, F32)
    ab_sel = ab_ref[:, :SEL_TILE]
    dm_sel = dm_ref[:, :SEL_TILE]

    def sel_tile(tile, carry):
        kb = pl.multiple_of(tile * SEL_TILE, SEL_TILE)
        s = _dot_nt(q_aug, ks_ref[pl.ds(kb, SEL_TILE), :])
        off = (q0 - kb).astype(F32)
        s = jnp.where(dm_sel <= off, s + ab_sel, NEG)
        c = slope * (-off)
        m_old = m_sc[...]
        m_new = jnp.maximum(m_old, jnp.max(s, axis=1, keepdims=True) + c)
        p = jnp.exp(s - (m_new - c))
        pv = jnp.dot(p.astype(BF16), vs_ref[pl.ds(kb, SEL_TILE), :], preferred_element_type=F32)
        acc_sc[...] = jnp.exp(m_old - m_new) * acc_sc[...] + pv
        m_sc[...] = m_new
        return carry

    lax.fori_loop(0, q0 // SEL_TILE + 1, sel_tile, 0)
    acc = acc_sc[...]
    o_s = acc[:, :HEAD_DIM] / acc[:, HEAD_DIM:HEAD_DIM + 1]

    kb_w = pl.multiple_of(jnp.maximum(q0 - WINDOW, 0), Q_BLOCK)
    off_w = (q0 - kb_w).astype(F32)
    dm = dm_ref[...]
    valid_w = (dm <= off_w) & (dm > off_w - float(WINDOW))
    sw = jnp.where(valid_w, _dot_nt(q, kw_ref[pl.ds(kb_w, WIN_KEYS), :]) + ab_ref[...], NEG)
    pw = jnp.exp(sw - jnp.max(sw, axis=1, keepdims=True))
    accw = jnp.dot(pw.astype(BF16), vw_ref[pl.ds(kb_w, WIN_KEYS), :], preferred_element_type=F32)
    o_w = accw[:, :HEAD_DIM] / accw[:, HEAD_DIM:HEAD_DIM + 1]

    gate = _sigmoid(gl_ref[...])
    o_ref[...] = gate[:, 0:1] * o_c + gate[:, 1:2] * o_s + gate[:, 2:3] * o_w


def _nsa(q_hp, gl, kc, vc, ks_ext, vs_ext, kw_ext, vw_ext, ab, abc, dm, slope_col, ovt):
    b, g, nq = q_hp.shape[:3]
    s = ks_ext.shape[2]
    ncp = kc.shape[2]
    step = lambda shp: pl.BlockSpec((None, None, None) + shp, lambda bi, gi, i: (bi, gi, i, 0, 0))
    per_bg = lambda shp: pl.BlockSpec((None, None) + shp, lambda bi, gi, i: (bi, gi, 0, 0))
    per_g = lambda shp: pl.BlockSpec((None,) + shp, lambda bi, gi, i: (gi, 0, 0))
    const = lambda shp: pl.BlockSpec(shp, lambda bi, gi, i: (0, 0))
    return pl.pallas_call(
        _nsa_body,
        grid=(b, g, nq),
        in_specs=[step((QROWS, LANES)), step((QROWS, 3)),
                  per_bg((ncp, LANES)), per_bg((ncp, LANES)),
                  per_bg((s, ks_ext.shape[3])), per_bg((s, LANES)),
                  per_bg((s, LANES)), per_bg((s, LANES)),
                  per_g((QROWS, WIN_KEYS)), per_g((QROWS, ncp)),
                  const((QROWS, WIN_KEYS)), per_g((QROWS, 1)), const(ovt.shape)],
        out_specs=step((QROWS, HEAD_DIM)),
        out_shape=jax.ShapeDtypeStruct((b, g, nq, QROWS, HEAD_DIM), F32),
        scratch_shapes=[pltpu.VMEM((QROWS, 1), F32), pltpu.VMEM((QROWS, LANES), F32)],
        compiler_params=_cparams(3),
    )(q_hp, gl, kc, vc, ks_ext, vs_ext, kw_ext, vw_ext, ab, abc, dm, slope_col, ovt)


def _gelu(x):
    return 0.5 * x * (1.0 + jnp.tanh(np.float32(np.sqrt(2.0 / np.pi)) * (x + 0.044715 * (x * x * x))))


def _sgu_body(uv_ref, zb_ref, lng_ref, lnb_ref, w_ref, bs_ref, o_ref):
    bw = zb_ref.shape[1]
    gd = bw // B_GROUPS
    u = _gelu(uv_ref[:, :bw])
    v = _gelu(uv_ref[:, bw:])
    mu = jnp.mean(v, axis=-1, keepdims=True)
    vc = v - mu
    vn = vc * lax.rsqrt(jnp.mean(vc * vc, axis=-1, keepdims=True) + EPS)
    vn = (vn * lng_ref[...] + lnb_ref[...]).astype(BF16)
    zb = zb_ref[...]
    gate = u * (zb * _sigmoid(zb))
    ti = lax.broadcasted_iota(jnp.int32, (B_CHUNK, B_CHUNK), 0)
    si = lax.broadcasted_iota(jnp.int32, (B_CHUNK, B_CHUNK), 1)
    for gi in range(B_GROUPS):
        wg = jnp.where(si <= ti, w_ref[gi], 0.0).astype(BF16)
        bcol = bs_ref[:, gi:gi + 1]
        for c in range(uv_ref.shape[0] // B_CHUNK):
            rs = slice(c * B_CHUNK, (c + 1) * B_CHUNK)
            cs = slice(gi * gd, (gi + 1) * gd)
            sv = jnp.dot(wg, vn[rs, cs], preferred_element_type=F32) + bcol
            o_ref[rs, cs] = (gate[rs, cs] * sv).astype(o_ref.dtype)


def _sgu(uv, zb, ln_g, ln_b, w_s, b_s):
    m, bw2 = uv.shape
    bw = bw2 // 2
    return pl.pallas_call(
        _sgu_body,
        grid=(m // ROW_TILE,),
        in_specs=[pl.BlockSpec((ROW_TILE, bw2), lambda i: (i, 0)),
                  pl.BlockSpec((ROW_TILE, bw), lambda i: (i, 0)),
                  pl.BlockSpec((1, bw), lambda i: (0, 0)),
                  pl.BlockSpec((1, bw), lambda i: (0, 0)),
                  pl.BlockSpec(w_s.shape, lambda i: (0, 0, 0)),
                  pl.BlockSpec((B_CHUNK, B_GROUPS), lambda i: (0, 0))],
        out_specs=pl.BlockSpec((ROW_TILE, bw), lambda i: (i, 0)),
        out_shape=jax.ShapeDtypeStruct((m, bw), BF16),
        compiler_params=_cparams(1),
    )(uv, zb, ln_g.reshape(1, bw), ln_b.reshape(1, bw), w_s, b_s.T)


def _merge_body(oa_ref, za_ref, ob_ref, ml_ref, wa_ref, wb_ref, o_ref):
    d = o_ref.shape[1]
    za = za_ref[...]
    oa = (oa_ref[...] * (za * _sigmoid(za))).astype(BF16)
    ua = jnp.dot(oa, wa_ref[...], preferred_element_type=F32)
    ub = jnp.dot(ob_ref[...], wb_ref[...], preferred_element_type=F32)
    o_ref[...] = (_sigmoid(ml_ref[:, :d]) * ua + _sigmoid(ml_ref[:, d:]) * ub).astype(o_ref.dtype)


def _merge(oa, za, ob, ml, w_up_a, w_up_b):
    m, aw = oa.shape
    bw = ob.shape[1]
    d = w_up_a.shape[1]
    tm = ROW_TILE // 2
    row = lambda w: pl.BlockSpec((tm, w), lambda i: (i, 0))
    return pl.pallas_call(
        _merge_body,
        grid=(m // tm,),
        in_specs=[row(aw), row(aw), row(bw), row(2 * d),
                  pl.BlockSpec((aw, d), lambda i: (0, 0)),
                  pl.BlockSpec((bw, d), lambda i: (0, 0))],
        out_specs=row(d),
        out_shape=jax.ShapeDtypeStruct((m, d), BF16),
        compiler_params=_cparams(1),
    )(oa, za, ob, ml, w_up_a, w_up_b)


def _out_body(x_ref, mg_ref, p_ref, wo_ref, wg_ref, wp_ref, fg_ref, o_ref):
    x1 = x_ref[...] + jnp.dot(mg_ref[...], wo_ref[...], preferred_element_type=F32)
    gl = jnp.dot(x1.astype(BF16), wg_ref[...], preferred_element_type=F32)
    ple = jnp.dot(p_ref[...].astype(BF16), wp_ref[...], preferred_element_type=F32)
    x2 = x1 + _sigmoid(gl) * ple
    y = x2 * lax.rsqrt(jnp.mean(x2 * x2, axis=-1, keepdims=True) + EPS)
    o_ref[...] = y * fg_ref[...]


def _out(x2d, merged, p2d, w_out, w_gate, w_ple, final_g):
    m, d = x2d.shape
    pd = p2d.shape[1]
    tm = ROW_TILE // 2
    row = lambda w: pl.BlockSpec((tm, w), lambda i: (i, 0))
    return pl.pallas_call(
        _out_body,
        grid=(m // tm,),
        in_specs=[row(d), row(d), row(pd),
                  pl.BlockSpec((d, d), lambda i: (0, 0)),
                  pl.BlockSpec((d, d), lambda i: (0, 0)),
                  pl.BlockSpec((pd, d), lambda i: (0, 0)),
                  pl.BlockSpec((1, d), lambda i: (0, 0))],
        out_specs=row(d),
        out_shape=jax.ShapeDtypeStruct((m, d), F32),
        compiler_params=_cparams(1),
    )(x2d, merged, p2d, w_out, w_gate, w_ple, final_g.reshape(1, d))


def _attention_constants(s, ncp):
    hh = np.arange(1, A_HEADS + 1, dtype=np.float32)
    slopes = np.power(np.float32(2.0), -8.0 * hh / A_HEADS).astype(np.float32).reshape(A_KV_GROUPS, A_REP)
    slope_col = np.repeat(slopes, Q_BLOCK, axis=1)[:, :, None]
    qi = (np.arange(QROWS) % Q_BLOCK).astype(np.float32)
    dm = np.arange(WIN_KEYS, dtype=np.float32)[None, :] - qi[:, None]
    ab = slope_col * dm[None]
    c_end = np.arange(ncp, dtype=np.float32) * CMP_STRIDE + (CMP_LEN - 1)
    abc = slope_col * (c_end[None, :] - qi[:, None])[None]
    ns = s // SEL_BLOCK
    ci = np.arange(ncp)[None, :]
    sj = np.arange(ns)[:, None]
    ovt = ((CMP_STRIDE * ci < SEL_BLOCK * sj + SEL_BLOCK) &
           (CMP_STRIDE * ci + CMP_LEN > SEL_BLOCK * sj) &
           (ci < ncp - 1)).astype(np.float32)
    onehot = (np.arange(s)[:, None] // SEL_BLOCK == np.arange(ns)[None, :]).astype(np.float32)
    return (jnp.asarray(ab), jnp.asarray(abc), jnp.asarray(dm), jnp.asarray(slope_col),
            jnp.asarray(ovt, dtype=BF16), jnp.asarray(onehot, dtype=BF16))


def _layer(x, p_l, norm_g, w_in, cmp_pe_k, cmp_w1_k, cmp_w2_k, cmp_pe_v, cmp_w1_v, cmp_w2_v,
           ln_v_g, ln_v_b, sgu_w, sgu_b, w_up_a, w_up_b, w_out, w_ple, w_ple_gate):
    b, s, d = x.shape
    m = b * s
    g, r, dh = A_KV_GROUPS, A_REP, HEAD_DIM
    nq = s // Q_BLOCK
    ncp = s // CMP_STRIDE
    b_width = w_up_b.shape[0]
    x2d = x.reshape(m, d)

    o_q, o_kv, o_gl = 0, A_WIDTH, A_WIDTH + 6 * A_KV_WIDTH
    o_za = o_gl + 3 * A_HEADS
    o_uv = o_za + A_WIDTH
    o_zb = o_uv + 2 * b_width
    o_ml = o_zb + b_width
    wb = w_in.astype(BF16)
    w_q = (w_in[:, o_q:o_kv] * (dh ** -0.5)).astype(BF16)
    w_kvc = wb[:, o_kv:o_kv + 2 * A_KV_WIDTH]
    w_kvsw = wb[:, o_kv + 2 * A_KV_WIDTH:o_gl]
    w_gl = jnp.pad(wb[:, o_gl:o_za], ((0, 0), (0, LANES - 3 * A_HEADS)))

    h = _rmsnorm(x2d, norm_g)
    q = _project(h, w_q, BF16)
    kvc = _project(h, w_kvc, F32)
    kvsw = _project(h, w_kvsw, BF16)
    glog = _project(h, w_gl, F32)[:, :3 * A_HEADS]
    za = _project(h, wb[:, o_za:o_uv], F32)
    uv = _project(h, wb[:, o_uv:o_zb], F32)
    zb = _project(h, wb[:, o_zb:o_ml], F32)
    ml = _project(h, wb[:, o_ml:], F32)

    heads_first = lambda t: t.reshape(b, s, g, dh).transpose(0, 2, 1, 3)
    ck = heads_first(kvc[:, :A_KV_WIDTH]).reshape(b, g, ncp, CMP_STRIDE * dh)
    cv = heads_first(kvc[:, A_KV_WIDTH:]).reshape(b, g, ncp, CMP_STRIDE * dh)
    kc, vc = _compress(ck, cv, cmp_pe_k, cmp_w1_k, cmp_w2_k, cmp_pe_v, cmp_w1_v, cmp_w2_v)

    ab, abc, dm, slope_col, ovt, onehot = _attention_constants(s, ncp)
    k_s, v_s, k_w, v_w = [heads_first(kvsw[:, j * A_KV_WIDTH:(j + 1) * A_KV_WIDTH]) for j in range(4)]
    zpad = jnp.zeros((b, g, s, LANES - dh), BF16)
    ones_col = jnp.zeros((b, g, s, LANES - dh), BF16).at[..., 0].set(1.0)
    ks_ext = jnp.concatenate([jnp.broadcast_to(onehot, (b, g) + onehot.shape), k_s, zpad], axis=-1)
    vs_ext = jnp.concatenate([v_s, ones_col], axis=-1)
    kw_ext = jnp.concatenate([k_w, zpad], axis=-1)
    vw_ext = jnp.concatenate([v_w, ones_col], axis=-1)
    q_hp = q.reshape(b, nq, Q_BLOCK, g, r, dh).transpose(0, 3, 1, 4, 2, 5)
    q_hp = jnp.pad(q_hp, ((0, 0),) * 5 + ((0, LANES - dh),)).reshape(b, g, nq, QROWS, LANES)
    gl = glog.reshape(b, nq, Q_BLOCK, 3, g, r).transpose(0, 4, 1, 5, 2, 3).reshape(b, g, nq, QROWS, 3)

    o_att = _nsa(q_hp, gl, kc, vc, ks_ext, vs_ext, kw_ext, vw_ext, ab, abc, dm, slope_col, ovt)
    o_att = o_att.reshape(b, g, nq, r, Q_BLOCK, dh).transpose(0, 2, 4, 1, 3, 5).reshape(m, A_WIDTH)

    ob = _sgu(uv, zb, ln_v_g, ln_v_b, sgu_w, sgu_b)
    merged = _merge(o_att, za, ob, ml, w_up_a.astype(BF16), w_up_b.astype(BF16))
    return x2d, merged


def kernel(x, p, norm_g, w_in, cmp_pe_k, cmp_w1_k, cmp_w2_k, cmp_pe_v, cmp_w1_v, cmp_w2_v, ln_v_g, ln_v_b, sgu_w, sgu_b, w_up_a, w_up_b, w_out, w_ple, w_ple_gate, final_g):
    b, s, d = x.shape
    depth = p.shape[0]
    assert depth == 1, "the final norm is fused into the (single) layer's output kernel"
    x2d, merged = _layer(x, p[0], norm_g[0], w_in[0], cmp_pe_k[0], cmp_w1_k[0], cmp_w2_k[0],
                         cmp_pe_v[0], cmp_w1_v[0], cmp_w2_v[0], ln_v_g[0], ln_v_b[0], sgu_w[0], sgu_b[0],
                         w_up_a[0], w_up_b[0], w_out[0], w_ple[0], w_ple_gate[0])
    out = _out(x2d, merged, p[0].reshape(b * s, -1), w_out[0].astype(BF16), w_ple_gate[0].astype(BF16),
               w_ple[0].astype(BF16), final_g)
    return out.reshape(b, s, d)
```

```python
import functools

import numpy as np
import jax
import jax.numpy as jnp
from jax import lax
from jax.experimental import pallas as pl
from jax.experimental.pallas import tpu as pltpu

A_HEADS = 16
A_KV_GROUPS = 4
A_REP = A_HEADS // A_KV_GROUPS
HEAD_DIM = 64
A_WIDTH = A_HEADS * HEAD_DIM
A_KV_WIDTH = A_KV_GROUPS * HEAD_DIM
CMP_LEN = 32
CMP_STRIDE = 16
SEL_BLOCK = 64
SEL_TOP_N = 16
WINDOW = 512
Q_BLOCK = 128
FORCE_BONUS = 1000.0
B_GROUPS = 8
B_CHUNK = 128
NEG = -1e30
EPS = 1e-6
LOG2E = float(np.log2(np.e))

LANES = 128
QROWS = A_REP * Q_BLOCK
SEL_TILE = 1024
SEL_CHUNKS = SEL_TILE // LANES
WIN_KEYS = WINDOW + Q_BLOCK
ROW_TILE = 512
VMEM_LIMIT = 56 * 1024 * 1024
POS_SHIFT = 5
POS_LO = 1 << POS_SHIFT
N_SLOPE_PARTS = 3

F32 = jnp.float32
BF16 = jnp.bfloat16


def _cparams(n_axes):
    return pltpu.CompilerParams(dimension_semantics=("arbitrary",) * n_axes,
                                vmem_limit_bytes=VMEM_LIMIT)


def _sigmoid(x):
    return 1.0 / (1.0 + jnp.exp(-x))


def _dot_nt(a, b):
    return lax.dot_general(a, b, (((1,), (1,)), ((), ())), preferred_element_type=F32)


def _row_tile_spec(width):
    return pl.BlockSpec((ROW_TILE, width), lambda i: (i, 0))


def _resident_spec(shape):
    return pl.BlockSpec(shape, lambda i: (0,) * len(shape))


def _norm_body(x_ref, g_ref, h_ref):
    x = x_ref[...]
    y = x * lax.rsqrt(jnp.mean(x * x, axis=-1, keepdims=True) + EPS)
    h_ref[...] = (y * g_ref[...]).astype(h_ref.dtype)


def _rmsnorm(x2d, g):
    m, d = x2d.shape
    return pl.pallas_call(
        _norm_body,
        grid=(m // ROW_TILE,),
        in_specs=[_row_tile_spec(d), _resident_spec((1, d))],
        out_specs=_row_tile_spec(d),
        out_shape=jax.ShapeDtypeStruct((m, d), BF16),
        compiler_params=_cparams(1),
    )(x2d, g.reshape(1, d))


def _mm_body(h_ref, w_ref, o_ref):
    o_ref[...] = jnp.dot(h_ref[...], w_ref[...], preferred_element_type=F32).astype(o_ref.dtype)


def _project(h, w, out_dtype):
    m, k = h.shape
    n = w.shape[1]
    tn = min(n, 2048)
    return pl.pallas_call(
        _mm_body,
        grid=(n // tn, m // ROW_TILE),
        in_specs=[pl.BlockSpec((ROW_TILE, k), lambda j, i: (i, 0)),
                  pl.BlockSpec((k, tn), lambda j, i: (0, j))],
        out_specs=pl.BlockSpec((ROW_TILE, tn), lambda j, i: (i, j)),
        out_shape=jax.ShapeDtypeStruct((m, n), out_dtype),
        compiler_params=_cparams(2),
    )(h, w)


def _q_body(h_ref, w_ref, qc_ref, o_ref):
    res = jnp.dot(h_ref[...], w_ref[...], preferred_element_type=F32)
    rows = res.shape[0]
    for hd in range(A_HEADS):
        cs = slice(hd * HEAD_DIM, (hd + 1) * HEAD_DIM)
        cst = jnp.broadcast_to(qc_ref[:, cs], (rows, HEAD_DIM))
        o_ref[:, hd * LANES:(hd + 1) * LANES] = jnp.concatenate([res[:, cs].astype(BF16), cst], axis=1)


def _project_q(h, w_q, q_cols):
    m, k = h.shape
    return pl.pallas_call(
        _q_body,
        grid=(m // ROW_TILE,),
        in_specs=[_row_tile_spec(k), _resident_spec(w_q.shape), _resident_spec(q_cols.shape)],
        out_specs=_row_tile_spec(A_HEADS * LANES),
        out_shape=jax.ShapeDtypeStruct((m, A_HEADS * LANES), BF16),
        compiler_params=_cparams(1),
    )(h, w_q, q_cols)


def _kvsw_body(seq, h_ref, w_ref, ks_ref, vs_ref, kw_ref, vw_ref):
    res = jnp.dot(h_ref[...], w_ref[...], preferred_element_type=F32)
    rows = res.shape[0]
    tok0 = (pl.program_id(0) * rows) % seq
    pos = tok0 + lax.broadcasted_iota(jnp.int32, (rows, LANES), 0)
    lane = lax.broadcasted_iota(jnp.int32, (rows, LANES), 1)
    onehot = jnp.where((pos >> 6) == lane, 1.0, 0.0).astype(BF16)
    off = (tok0 + lax.broadcasted_iota(jnp.int32, (rows, HEAD_DIM), 0)) & (SEL_TILE - 1)
    lane_h = lax.broadcasted_iota(jnp.int32, (rows, HEAD_DIM), 1)
    pos_cols = jnp.where(lane_h < N_SLOPE_PARTS, off >> POS_SHIFT,
                         jnp.where(lane_h < 2 * N_SLOPE_PARTS, off & (POS_LO - 1), 0)).astype(F32).astype(BF16)
    ones_col = jnp.where(lane_h == 0, 1.0, 0.0).astype(BF16)
    zeros = jnp.zeros((rows, HEAD_DIM), BF16)
    piece = lambda j: res[:, j * HEAD_DIM:(j + 1) * HEAD_DIM].astype(BF16)
    for g in range(A_KV_GROUPS):
        ks_ref[g, :, 0:LANES] = onehot
        ks_ref[g, :, LANES:2 * LANES] = jnp.concatenate([piece(g), pos_cols], axis=1)
        vs_ref[g] = jnp.concatenate([piece(A_KV_GROUPS + g), ones_col], axis=1)
        kw_ref[g] = jnp.concatenate([piece(2 * A_KV_GROUPS + g), zeros], axis=1)
        vw_ref[g] = jnp.concatenate([piece(3 * A_KV_GROUPS + g), ones_col], axis=1)


def _project_kvsw(h, w, b, s):
    m, k = h.shape
    tps = s // ROW_TILE
    assert s // SEL_BLOCK <= LANES, "one-hot selection columns must fit one lane tile"
    spec = lambda width: pl.BlockSpec((None, A_KV_GROUPS, ROW_TILE, width),
                                      lambda i: (i // tps, 0, i % tps, 0))
    shape = lambda width: jax.ShapeDtypeStruct((b, A_KV_GROUPS, s, width), BF16)
    return pl.pallas_call(
        lambda *refs: _kvsw_body(s, *refs),
        grid=(m // ROW_TILE,),
        in_specs=[_row_tile_spec(k), _resident_spec(w.shape)],
        out_specs=[spec(2 * LANES), spec(LANES), spec(LANES), spec(LANES)],
        out_shape=[shape(2 * LANES), shape(LANES), shape(LANES), shape(LANES)],
        compiler_params=_cparams(1),
    )(h, w)


def _kvc_body(h_ref, w_ref, ck_ref, cv_ref):
    res = jnp.dot(h_ref[...], w_ref[...], preferred_element_type=F32)
    for g in range(A_KV_GROUPS):
        ck_ref[g] = res[:, g * HEAD_DIM:(g + 1) * HEAD_DIM]
        cv_ref[g] = res[:, (A_KV_GROUPS + g) * HEAD_DIM:(A_KV_GROUPS + g + 1) * HEAD_DIM]


def _project_kvc(h, w, b, s):
    m, k = h.shape
    tps = s // ROW_TILE
    spec = pl.BlockSpec((None, A_KV_GROUPS, ROW_TILE, HEAD_DIM), lambda i: (i // tps, 0, i % tps, 0))
    shape = jax.ShapeDtypeStruct((b, A_KV_GROUPS, s, HEAD_DIM), F32)
    return pl.pallas_call(
        _kvc_body,
        grid=(m // ROW_TILE,),
        in_specs=[_row_tile_spec(k), _resident_spec(w.shape)],
        out_specs=[spec, spec],
        out_shape=[shape, shape],
        compiler_params=_cparams(1),
    )(h, w)


def _compress_one(c, pe, w1, w2p, ncp):
    half = CMP_STRIDE * HEAD_DIM
    hp = lax.Precision.HIGHEST
    h_lo = jnp.dot(c + pe[:, :half], w1[:half, :], preferred_element_type=F32, precision=hp)
    h_hi = jnp.dot(c + pe[:, half:], w1[half:, :], preferred_element_type=F32, precision=hp)
    pre = h_lo + pltpu.roll(h_hi, ncp - 1, 0)
    hid = pre * _sigmoid(pre)
    out = jnp.dot(hid, w2p, preferred_element_type=F32, precision=hp)
    row = lax.broadcasted_iota(jnp.int32, out.shape, 0)
    return jnp.where(row < ncp - 1, out, 0.0)


def _compress_body(ck_ref, cv_ref, pek_ref, w1k_ref, w2k_ref, pev_ref, w1v_ref, w2v_ref,
                   kc_ref, vc_ref):
    ncp = ck_ref.shape[0]
    kc = _compress_one(ck_ref[...], pek_ref[...], w1k_ref[...], w2k_ref[...], ncp)
    vc = _compress_one(cv_ref[...], pev_ref[...], w1v_ref[...], w2v_ref[...], ncp)
    kc_ref[...] = kc.astype(kc_ref.dtype)
    vc_ref[...] = vc.astype(vc_ref.dtype)


def _compress(ck, cv, pe_k, w1_k, w2_k, pe_v, w1_v, w2_v):
    b, g, ncp, cw = ck.shape
    flat = CMP_LEN * HEAD_DIM
    pad2 = lambda w: jnp.pad(w, ((0, 0), (0, LANES - HEAD_DIM)))
    tok = pl.BlockSpec((None, None, ncp, cw), lambda bi, gi: (bi, gi, 0, 0))
    full = lambda shp: pl.BlockSpec(shp, lambda bi, gi: (0,) * len(shp))
    out = pl.BlockSpec((None, None, ncp, LANES), lambda bi, gi: (bi, gi, 0, 0))
    return pl.pallas_call(
        _compress_body,
        grid=(b, g),
        in_specs=[tok, tok,
                  full((1, flat)), full((flat, HEAD_DIM)), full((HEAD_DIM, LANES)),
                  full((1, flat)), full((flat, HEAD_DIM)), full((HEAD_DIM, LANES))],
        out_specs=[out, out],
        out_shape=[jax.ShapeDtypeStruct((b, g, ncp, LANES), BF16)] * 2,
        compiler_params=_cparams(2),
    )(ck, cv,
      pe_k.reshape(1, flat), w1_k.reshape(flat, HEAD_DIM), pad2(w2_k),
      pe_v.reshape(1, flat), w1_v.reshape(flat, HEAD_DIM), pad2(w2_v))


def _nsa_body(n_top, q_ref, gl_ref, kc_ref, vc_ref, ks_ref, vs_ref, kw_ref, vw_ref,
              ab_ref, abc_ref, dm_ref, slope_ref, ovt_ref, o_ref, m_sc, acc_sc):
    i = pl.program_id(2)
    q0 = i * Q_BLOCK
    qb = q_ref[...]
    q = jnp.concatenate([qb[:, r * LANES:(r + 1) * LANES] for r in range(A_REP)], axis=0)
    ncp = kc_ref.shape[0]
    n_sel = ovt_ref.shape[0]
    slope = slope_ref[...]

    rows = lax.broadcasted_iota(jnp.int32, (QROWS, ncp), 0)
    t_q = q0 + (rows & (Q_BLOCK - 1))
    c_end = lax.broadcasted_iota(jnp.int32, (QROWS, ncp), 1) * CMP_STRIDE + (CMP_LEN - 1)
    valid_c = t_q >= c_end
    sc = jnp.where(valid_c, _dot_nt(q, kc_ref[...]) + abc_ref[...], NEG)
    mc = jnp.max(sc, axis=1, keepdims=True)
    pc = jnp.where(valid_c, jnp.exp2(sc - mc), 0.0)
    lc = jnp.sum(pc, axis=1, keepdims=True)
    pc = pc * jnp.where(lc > 0.0, 1.0 / lc, 0.0)
    o_c = jnp.dot(pc.astype(BF16), vc_ref[...], preferred_element_type=F32)[:, :HEAD_DIM]

    psum = pc[0:Q_BLOCK]
    for r in range(1, A_REP):
        psum = psum + pc[r * Q_BLOCK:(r + 1) * Q_BLOCK]
    p1 = psum.astype(BF16)
    r1 = psum - p1.astype(F32)
    p2 = r1.astype(BF16)
    p3 = (r1 - p2.astype(F32)).astype(BF16)
    ovt = ovt_ref[...]
    imp_t = _dot_nt(ovt, p3) + _dot_nt(ovt, p2) + _dot_nt(ovt, p1)

    kb_w = pl.multiple_of(jnp.maximum(q0 - WINDOW, 0), Q_BLOCK)
    off_w = (q0 - kb_w).astype(F32)
    dm = dm_ref[...]
    valid_w = (dm <= off_w) & (dm > off_w - float(WINDOW))
    sw = jnp.where(valid_w, _dot_nt(q, kw_ref[pl.ds(kb_w, WIN_KEYS), :]) + ab_ref[...], NEG)
    pw = jnp.exp2(sw - jnp.max(sw, axis=1, keepdims=True))
    accw = jnp.dot(pw.astype(BF16), vw_ref[pl.ds(kb_w, WIN_KEYS), :], preferred_element_type=F32)
    o_w = accw[:, :HEAD_DIM] / accw[:, HEAD_DIM:HEAD_DIM + 1]

    q0a = pl.multiple_of(q0, Q_BLOCK)
    kb_d = (q0 // SEL_TILE) * SEL_TILE
    c_d = slope * (kb_d - q0).astype(F32)
    qi = lax.broadcasted_iota(jnp.int32, (QROWS, Q_BLOCK), 0) & (Q_BLOCK - 1)
    kj = lax.broadcasted_iota(jnp.int32, (QROWS, Q_BLOCK), 1)
    s_d = jnp.where(kj <= qi, _dot_nt(q, ks_ref[pl.ds(q0a, Q_BLOCK), LANES:2 * LANES]), NEG)
    mx_d = jnp.broadcast_to(jnp.max(s_d, axis=1, keepdims=True), (QROWS, LANES))
    p_d = jnp.exp2(s_d - mx_d)
    acc_sc[...] = jnp.dot(p_d.astype(BF16), vs_ref[pl.ds(q0a, Q_BLOCK), :], preferred_element_type=F32)
    m_sc[...] = mx_d + c_d

    blk = lax.broadcasted_iota(jnp.int32, (n_sel, Q_BLOCK), 0)
    t_l = q0 + lax.broadcasted_iota(jnp.int32, (n_sel, Q_BLOCK), 1)
    cur = t_l >> 6
    causal_b = blk * SEL_BLOCK <= t_l
    forced = (blk == 0) | (blk == cur) | (blk == cur - 1)
    score = jnp.where(causal_b, imp_t + jnp.where(forced, FORCE_BONUS, 0.0), -1.0)
    blk_f = blk.astype(F32)
    sel = jnp.zeros((n_sel, Q_BLOCK), F32)
    for _ in range(n_top):
        top = jnp.max(score, axis=0, keepdims=True)
        first = jnp.min(jnp.where(score == top, blk_f, float(n_sel)), axis=0, keepdims=True)
        pick = blk_f == first
        sel = jnp.where(pick, 1.0, sel)
        score = jnp.where(pick, -3e38, score)
    bias_t = jnp.where((sel > 0.0) & (blk * SEL_BLOCK < q0), 0.0, NEG)
    bias = bias_t.T.astype(BF16)
    q_aug = jnp.concatenate([jnp.concatenate([bias] * A_REP, axis=0), q], axis=1)

    def sel_tile(tile, carry):
        kb = pl.multiple_of(tile * SEL_TILE, SEL_TILE)
        s = _dot_nt(q_aug, ks_ref[pl.ds(kb, SEL_TILE), :])
        c = slope * (kb - q0).astype(F32)
        mx = s[:, 0:LANES]
        for j in range(1, SEL_CHUNKS):
            mx = jnp.maximum(mx, s[:, j * LANES:(j + 1) * LANES])
        m_old = m_sc[...]
        m_new = jnp.maximum(m_old, jnp.max(mx, axis=1, keepdims=True) + c)
        shift = m_new - c
        p = jnp.concatenate([jnp.exp2(s[:, j * LANES:(j + 1) * LANES] - shift).astype(BF16)
                             for j in range(SEL_CHUNKS)], axis=1)
        pv = jnp.dot(p, vs_ref[pl.ds(kb, SEL_TILE), :], preferred_element_type=F32)
        acc_sc[...] = jnp.exp2(m_old - m_new) * acc_sc[...] + pv
        m_sc[...] = m_new
        return carry

    lax.fori_loop(0, (q0 + SEL_TILE - 1) // SEL_TILE, sel_tile, 0)
    acc = acc_sc[...]
    o_s = acc[:, :HEAD_DIM] / acc[:, HEAD_DIM:HEAD_DIM + 1]

    gate = _sigmoid(gl_ref[...])
    o = gate[:, 0:1] * o_c + gate[:, 1:2] * o_s + gate[:, 2:3] * o_w
    for r in range(A_REP):
        o_ref[:, r * HEAD_DIM:(r + 1) * HEAD_DIM] = o[r * Q_BLOCK:(r + 1) * Q_BLOCK]


def _nsa(q_pad, gl, kc, vc, ks_ext, vs_ext, kw_ext, vw_ext, ab, abc, dm, slope_b, ovt):
    b, g, s = ks_ext.shape[:3]
    nq = s // Q_BLOCK
    ncp = kc.shape[2]
    per_bg = lambda shp: pl.BlockSpec((None, None) + shp, lambda bi, gi, i: (bi, gi, 0, 0))
    per_g = lambda shp: pl.BlockSpec((None,) + shp, lambda bi, gi, i: (gi, 0, 0))
    const = lambda shp: pl.BlockSpec(shp, lambda bi, gi, i: (0, 0))
    return pl.pallas_call(
        functools.partial(_nsa_body, min(SEL_TOP_N, s // SEL_BLOCK)),
        grid=(b, g, nq),
        in_specs=[pl.BlockSpec((Q_BLOCK, A_REP * LANES), lambda bi, gi, i: (bi * nq + i, gi)),
                  pl.BlockSpec((None, None, None, QROWS, 3), lambda bi, gi, i: (bi, gi, i, 0, 0)),
                  per_bg((ncp, LANES)), per_bg((ncp, LANES)),
                  per_bg((s, ks_ext.shape[3])), per_bg((s, LANES)),
                  per_bg((s, LANES)), per_bg((s, LANES)),
                  per_g((QROWS, WIN_KEYS)), per_g((QROWS, ncp)),
                  const((QROWS, WIN_KEYS)), per_g((QROWS, LANES)), const(ovt.shape)],
        out_specs=pl.BlockSpec((None, Q_BLOCK, A_REP * HEAD_DIM), lambda bi, gi, i: (bi, i, gi)),
        out_shape=jax.ShapeDtypeStruct((b, s, A_WIDTH), F32),
        scratch_shapes=[pltpu.VMEM((QROWS, LANES), F32), pltpu.VMEM((QROWS, LANES), F32)],
        compiler_params=_cparams(3),
    )(q_pad, gl, kc, vc, ks_ext, vs_ext, kw_ext, vw_ext, ab, abc, dm, slope_b, ovt)


def _gelu(x):
    return 0.5 * x * (1.0 + jnp.tanh(np.float32(np.sqrt(2.0 / np.pi)) * (x + 0.044715 * (x * x * x))))


def _sgu_body(uv_ref, zb_ref, lng_ref, lnb_ref, w_ref, bs_ref, o_ref):
    bw = zb_ref.shape[1]
    gd = bw // B_GROUPS
    u = _gelu(uv_ref[:, :bw])
    v = _gelu(uv_ref[:, bw:])
    mu = jnp.mean(v, axis=-1, keepdims=True)
    vc = v - mu
    vn = vc * lax.rsqrt(jnp.mean(vc * vc, axis=-1, keepdims=True) + EPS)
    vn = (vn * lng_ref[...] + lnb_ref[...]).astype(BF16)
    zb = zb_ref[...]
    gate = u * (zb * _sigmoid(zb))
    ti = lax.broadcasted_iota(jnp.int32, (B_CHUNK, B_CHUNK), 0)
    si = lax.broadcasted_iota(jnp.int32, (B_CHUNK, B_CHUNK), 1)
    for gi in range(B_GROUPS):
        wg = jnp.where(si <= ti, w_ref[gi], 0.0).astype(BF16)
        bcol = bs_ref[:, gi:gi + 1]
        for c in range(uv_ref.shape[0] // B_CHUNK):
            rs = slice(c * B_CHUNK, (c + 1) * B_CHUNK)
            cs = slice(gi * gd, (gi + 1) * gd)
            sv = jnp.dot(wg, vn[rs, cs], preferred_element_type=F32) + bcol
            o_ref[rs, cs] = (gate[rs, cs] * sv).astype(o_ref.dtype)


def _sgu(uv, zb, ln_g, ln_b, w_s, b_s):
    m, bw2 = uv.shape
    bw = bw2 // 2
    return pl.pallas_call(
        _sgu_body,
        grid=(m // ROW_TILE,),
        in_specs=[_row_tile_spec(bw2), _row_tile_spec(bw),
                  _resident_spec((1, bw)), _resident_spec((1, bw)),
                  _resident_spec(w_s.shape), _resident_spec((B_CHUNK, B_GROUPS))],
        out_specs=_row_tile_spec(bw),
        out_shape=jax.ShapeDtypeStruct((m, bw), BF16),
        compiler_params=_cparams(1),
    )(uv, zb, ln_g.reshape(1, bw), ln_b.reshape(1, bw), w_s, b_s.T)


def _merge_body(oa_ref, za_ref, ob_ref, ml_ref, wa_ref, wb_ref, o_ref):
    d = o_ref.shape[1]
    za = za_ref[...]
    oa = (oa_ref[...] * (za * _sigmoid(za))).astype(BF16)
    ua = jnp.dot(oa, wa_ref[...], preferred_element_type=F32)
    ub = jnp.dot(ob_ref[...], wb_ref[...], preferred_element_type=F32)
    o_ref[...] = (_sigmoid(ml_ref[:, :d]) * ua + _sigmoid(ml_ref[:, d:]) * ub).astype(o_ref.dtype)


def _merge(oa, za, ob, ml, w_up_a, w_up_b):
    m, aw = oa.shape
    bw = ob.shape[1]
    d = w_up_a.shape[1]
    tm = ROW_TILE // 2
    row = lambda w: pl.BlockSpec((tm, w), lambda i: (i, 0))
    return pl.pallas_call(
        _merge_body,
        grid=(m // tm,),
        in_specs=[row(aw), row(aw), row(bw), row(2 * d),
                  _resident_spec((aw, d)), _resident_spec((bw, d))],
        out_specs=row(d),
        out_shape=jax.ShapeDtypeStruct((m, d), BF16),
        compiler_params=_cparams(1),
    )(oa, za, ob, ml, w_up_a, w_up_b)


def _out_body(x_ref, mg_ref, p_ref, wo_ref, wg_ref, wp_ref, fg_ref, o_ref):
    x1 = x_ref[...] + jnp.dot(mg_ref[...], wo_ref[...], preferred_element_type=F32)
    gl = jnp.dot(x1.astype(BF16), wg_ref[...], preferred_element_type=F32)
    ple = jnp.dot(p_ref[...].astype(BF16), wp_ref[...], preferred_element_type=F32)
    x2 = x1 + _sigmoid(gl) * ple
    y = x2 * lax.rsqrt(jnp.mean(x2 * x2, axis=-1, keepdims=True) + EPS)
    o_ref[...] = y * fg_ref[...]


def _out(x2d, merged, p2d, w_out, w_gate, w_ple, final_g):
    m, d = x2d.shape
    pd = p2d.shape[1]
    tm = ROW_TILE // 2
    row = lambda w: pl.BlockSpec((tm, w), lambda i: (i, 0))
    return pl.pallas_call(
        _out_body,
        grid=(m // tm,),
        in_specs=[row(d), row(d), row(pd),
                  _resident_spec((d, d)), _resident_spec((d, d)), _resident_spec((pd, d)),
                  _resident_spec((1, d))],
        out_specs=row(d),
        out_shape=jax.ShapeDtypeStruct((m, d), F32),
        compiler_params=_cparams(1),
    )(x2d, merged, p2d, w_out, w_gate, w_ple, final_g.reshape(1, d))


def _bf16_parts(x, n):
    parts, rest = [], jnp.asarray(x, F32)
    for _ in range(n):
        piece = rest.astype(BF16)
        parts.append(piece)
        rest = rest - piece.astype(F32)
    return parts


def _attention_constants(s, ncp):
    hh = np.arange(1, A_HEADS + 1, dtype=np.float32)
    slopes = np.power(np.float32(2.0), -8.0 * hh / A_HEADS).astype(np.float32)
    slopes2 = (slopes * np.float32(LOG2E)).astype(np.float32)
    slope_col = np.repeat(slopes2.reshape(A_KV_GROUPS, A_REP), Q_BLOCK, axis=1)[:, :, None]
    slope_b = np.broadcast_to(slope_col, (A_KV_GROUPS, QROWS, LANES)).copy()
    qi = (np.arange(QROWS) % Q_BLOCK).astype(np.float32)
    dm = np.arange(WIN_KEYS, dtype=np.float32)[None, :] - qi[:, None]
    ab = slope_col * dm[None]
    c_end = np.arange(ncp, dtype=np.float32) * CMP_STRIDE + (CMP_LEN - 1)
    abc = slope_col * (c_end[None, :] - qi[:, None])[None]
    ci = np.arange(ncp)[None, :]
    sj = np.arange(LANES)[:, None]
    ovt = ((CMP_STRIDE * ci < SEL_BLOCK * sj + SEL_BLOCK) &
           (CMP_STRIDE * ci + CMP_LEN > SEL_BLOCK * sj) &
           (ci < ncp - 1)).astype(np.float32)
    parts = _bf16_parts(slopes2, N_SLOPE_PARTS)
    cols = jnp.stack([pp * POS_LO for pp in parts] + parts, axis=1)
    q_cols = jnp.pad(cols, ((0, 0), (0, HEAD_DIM - cols.shape[1]))).reshape(1, A_HEADS * HEAD_DIM)
    return (jnp.asarray(ab), jnp.asarray(abc), jnp.asarray(dm), jnp.asarray(slope_b),
            jnp.asarray(ovt, dtype=BF16), q_cols)


def kernel(x, p, norm_g, w_in, cmp_pe_k, cmp_w1_k, cmp_w2_k, cmp_pe_v, cmp_w1_v, cmp_w2_v, ln_v_g, ln_v_b, sgu_w, sgu_b, w_up_a, w_up_b, w_out, w_ple, w_ple_gate, final_g):
    b, s, d = x.shape
    assert p.shape[0] == 1, "the final norm is fused into the (single) layer's output kernel"
    m = b * s
    g, r = A_KV_GROUPS, A_REP
    nq = s // Q_BLOCK
    ncp = s // CMP_STRIDE
    b_width = w_up_b.shape[1]
    x2d = x.reshape(m, d)
    w_in0 = w_in[0]

    o_kv = A_WIDTH
    o_gl = o_kv + 6 * A_KV_WIDTH
    o_za = o_gl + 3 * A_HEADS
    o_uv = o_za + A_WIDTH
    o_zb = o_uv + 2 * b_width
    o_ml = o_zb + b_width
    wcast = lambda lo, hi: w_in0[:, lo:hi].astype(BF16)
    w_q = (w_in0[:, :o_kv] * np.float32(HEAD_DIM ** -0.5 * LOG2E)).astype(BF16)
    w_gl = jnp.pad(wcast(o_gl, o_za), ((0, 0), (0, LANES - 3 * A_HEADS)))

    ab, abc, dm, slope_b, ovt, q_cols = _attention_constants(s, ncp)

    h = _rmsnorm(x2d, norm_g[0])
    q_pad = _project_q(h, w_q, q_cols)
    ck, cv = _project_kvc(h, wcast(o_kv, o_kv + 2 * A_KV_WIDTH), b, s)
    ks_ext, vs_ext, kw_ext, vw_ext = _project_kvsw(h, wcast(o_kv + 2 * A_KV_WIDTH, o_gl), b, s)
    glog = _project(h, w_gl, F32)[:, :3 * A_HEADS]
    za = _project(h, wcast(o_za, o_uv), F32)
    uv = _project(h, wcast(o_uv, o_zb), F32)
    zb = _project(h, wcast(o_zb, o_ml), F32)
    ml = _project(h, wcast(o_ml, w_in0.shape[1]), F32)

    cflat = (b, g, ncp, CMP_STRIDE * HEAD_DIM)
    kc, vc = _compress(ck.reshape(cflat), cv.reshape(cflat), cmp_pe_k[0], cmp_w1_k[0], cmp_w2_k[0],
                       cmp_pe_v[0], cmp_w1_v[0], cmp_w2_v[0])
    gl = glog.reshape(b, nq, Q_BLOCK, 3, g, r).transpose(0, 4, 1, 5, 2, 3).reshape(b, g, nq, QROWS, 3)
    o_att = _nsa(q_pad, gl, kc, vc, ks_ext, vs_ext, kw_ext, vw_ext, ab, abc, dm, slope_b, ovt)

    ob = _sgu(uv, zb, ln_v_g[0], ln_v_b[0], sgu_w[0], sgu_b[0])
    merged = _merge(o_att.reshape(m, A_WIDTH), za, ob, ml, w_up_a[0].astype(BF16), w_up_b[0].astype(BF16))
    out = _out(x2d, merged, p[0].reshape(m, -1), w_out[0].astype(BF16), w_ple_gate[0].astype(BF16),
               w_ple[0].astype(BF16), final_g)
    return out.reshape(b, s, d)
```

```python
import functools

import numpy as np
import jax
import jax.numpy as jnp
from jax import lax
from jax.experimental import pallas as pl
from jax.experimental.pallas import tpu as pltpu

A_HEADS = 16
A_KV_GROUPS = 4
A_REP = A_HEADS // A_KV_GROUPS
HEAD_DIM = 64
A_WIDTH = A_HEADS * HEAD_DIM
A_KV_WIDTH = A_KV_GROUPS * HEAD_DIM
CMP_LEN = 32
CMP_STRIDE = 16
SEL_BLOCK = 64
SEL_TOP_N = 16
WINDOW = 512
Q_BLOCK = 128
FORCE_BONUS = 1000.0
B_GROUPS = 8
B_CHUNK = 128
NEG = -1e30
EPS = 1e-6
LOG2E = float(np.log2(np.e))

LANES = 128
QROWS = A_REP * Q_BLOCK
SEL_TILE = 1024
SEL_CHUNKS = SEL_TILE // LANES
WIN_KEYS = WINDOW + Q_BLOCK
ROW_TILE = 512
VMEM_LIMIT = 56 * 1024 * 1024
POS_SHIFT = 5
POS_LO = 1 << POS_SHIFT
N_SLOPE_PARTS = 3

F32 = jnp.float32
BF16 = jnp.bfloat16


def _cparams(n_axes):
    return pltpu.CompilerParams(dimension_semantics=("arbitrary",) * n_axes,
                                vmem_limit_bytes=VMEM_LIMIT)


def _sigmoid(x):
    return 1.0 / (1.0 + jnp.exp(-x))


def _dot_nt(a, b):
    return lax.dot_general(a, b, (((1,), (1,)), ((), ())), preferred_element_type=F32)


def _row_tile_spec(width):
    return pl.BlockSpec((ROW_TILE, width), lambda i: (i, 0))


def _resident_spec(shape):
    return pl.BlockSpec(shape, lambda i: (0,) * len(shape))


def _norm_body(x_ref, g_ref, h_ref):
    x = x_ref[...]
    y = x * lax.rsqrt(jnp.mean(x * x, axis=-1, keepdims=True) + EPS)
    h_ref[...] = (y * g_ref[...]).astype(h_ref.dtype)


def _rmsnorm(x2d, g):
    m, d = x2d.shape
    return pl.pallas_call(
        _norm_body,
        grid=(m // ROW_TILE,),
        in_specs=[_row_tile_spec(d), _resident_spec((1, d))],
        out_specs=_row_tile_spec(d),
        out_shape=jax.ShapeDtypeStruct((m, d), BF16),
        compiler_params=_cparams(1),
    )(x2d, g.reshape(1, d))


def _mm_body(h_ref, w_ref, o_ref):
    o_ref[...] = jnp.dot(h_ref[...], w_ref[...], preferred_element_type=F32).astype(o_ref.dtype)


def _project(h, w, out_dtype):
    m, k = h.shape
    n = w.shape[1]
    tn = min(n, 2048)
    return pl.pallas_call(
        _mm_body,
        grid=(n // tn, m // ROW_TILE),
        in_specs=[pl.BlockSpec((ROW_TILE, k), lambda j, i: (i, 0)),
                  pl.BlockSpec((k, tn), lambda j, i: (0, j))],
        out_specs=pl.BlockSpec((ROW_TILE, tn), lambda j, i: (i, j)),
        out_shape=jax.ShapeDtypeStruct((m, n), out_dtype),
        compiler_params=_cparams(2),
    )(h, w)


def _q_body(h_ref, w_ref, qc_ref, o_ref):
    res = jnp.dot(h_ref[...], w_ref[...], preferred_element_type=F32)
    rows = res.shape[0]
    for hd in range(A_HEADS):
        cs = slice(hd * HEAD_DIM, (hd + 1) * HEAD_DIM)
        cst = jnp.broadcast_to(qc_ref[:, cs], (rows, HEAD_DIM))
        o_ref[:, hd * LANES:(hd + 1) * LANES] = jnp.concatenate([res[:, cs].astype(BF16), cst], axis=1)


def _project_q(h, w_q, q_cols):
    m, k = h.shape
    return pl.pallas_call(
        _q_body,
        grid=(m // ROW_TILE,),
        in_specs=[_row_tile_spec(k), _resident_spec(w_q.shape), _resident_spec(q_cols.shape)],
        out_specs=_row_tile_spec(A_HEADS * LANES),
        out_shape=jax.ShapeDtypeStruct((m, A_HEADS * LANES), BF16),
        compiler_params=_cparams(1),
    )(h, w_q, q_cols)


def _kvsw_body(seq, h_ref, w_ref, ks_ref, vs_ref, kw_ref, vw_ref):
    res = jnp.dot(h_ref[...], w_ref[...], preferred_element_type=F32)
    rows = res.shape[0]
    tok0 = (pl.program_id(0) * rows) % seq
    pos = tok0 + lax.broadcasted_iota(jnp.int32, (rows, LANES), 0)
    lane = lax.broadcasted_iota(jnp.int32, (rows, LANES), 1)
    onehot = jnp.where((pos >> 6) == lane, 1.0, 0.0).astype(BF16)
    off = (tok0 + lax.broadcasted_iota(jnp.int32, (rows, HEAD_DIM), 0)) & (SEL_TILE - 1)
    lane_h = lax.broadcasted_iota(jnp.int32, (rows, HEAD_DIM), 1)
    pos_cols = jnp.where(lane_h < N_SLOPE_PARTS, off >> POS_SHIFT,
                         jnp.where(lane_h < 2 * N_SLOPE_PARTS, off & (POS_LO - 1), 0)).astype(F32).astype(BF16)
    ones_col = jnp.where(lane_h == 0, 1.0, 0.0).astype(BF16)
    zeros = jnp.zeros((rows, HEAD_DIM), BF16)
    piece = lambda j: res[:, j * HEAD_DIM:(j + 1) * HEAD_DIM].astype(BF16)
    for g in range(A_KV_GROUPS):
        ks_ref[g, :, 0:LANES] = onehot
        ks_ref[g, :, LANES:2 * LANES] = jnp.concatenate([piece(g), pos_cols], axis=1)
        vs_ref[g] = jnp.concatenate([piece(A_KV_GROUPS + g), ones_col], axis=1)
        kw_ref[g] = jnp.concatenate([piece(2 * A_KV_GROUPS + g), zeros], axis=1)
        vw_ref[g] = jnp.concatenate([piece(3 * A_KV_GROUPS + g), ones_col], axis=1)


def _project_kvsw(h, w, b, s):
    m, k = h.shape
    tps = s // ROW_TILE
    assert s // SEL_BLOCK <= LANES, "one-hot selection columns must fit one lane tile"
    spec = lambda width: pl.BlockSpec((None, A_KV_GROUPS, ROW_TILE, width),
                                      lambda i: (i // tps, 0, i % tps, 0))
    shape = lambda width: jax.ShapeDtypeStruct((b, A_KV_GROUPS, s, width), BF16)
    return pl.pallas_call(
        lambda *refs: _kvsw_body(s, *refs),
        grid=(m // ROW_TILE,),
        in_specs=[_row_tile_spec(k), _resident_spec(w.shape)],
        out_specs=[spec(2 * LANES), spec(LANES), spec(LANES), spec(LANES)],
        out_shape=[shape(2 * LANES), shape(LANES), shape(LANES), shape(LANES)],
        compiler_params=_cparams(1),
    )(h, w)


def _kvc_body(h_ref, w_ref, ck_ref, cv_ref, res_sc):
    res = jnp.dot(h_ref[...], w_ref[...], preferred_element_type=F32)
    n_slabs = res_sc.shape[0]
    for c in range(n_slabs):
        res_sc[c] = res[:, c * LANES:(c + 1) * LANES]
    nrow = res_sc.shape[1] // CMP_STRIDE
    heads_per_slab = LANES // HEAD_DIM
    for c in range(n_slabs):
        out_ref = ck_ref if c < n_slabs // 2 else cv_ref
        g0 = (c % (n_slabs // 2)) * heads_per_slab
        for l in range(CMP_STRIDE):
            slab = res_sc[c, pl.ds(l, nrow, stride=CMP_STRIDE), :]
            for j in range(heads_per_slab):
                out_ref[g0 + j, :, l * HEAD_DIM:(l + 1) * HEAD_DIM] = slab[:, j * HEAD_DIM:(j + 1) * HEAD_DIM]


def _project_kvc(h, w, b, s):
    m, k = h.shape
    tps = s // ROW_TILE
    nrow = ROW_TILE // CMP_STRIDE
    width = CMP_STRIDE * HEAD_DIM
    spec = pl.BlockSpec((None, A_KV_GROUPS, nrow, width), lambda i: (i // tps, 0, i % tps, 0))
    shape = jax.ShapeDtypeStruct((b, A_KV_GROUPS, s // CMP_STRIDE, width), F32)
    return pl.pallas_call(
        _kvc_body,
        grid=(m // ROW_TILE,),
        in_specs=[_row_tile_spec(k), _resident_spec(w.shape)],
        out_specs=[spec, spec],
        out_shape=[shape, shape],
        scratch_shapes=[pltpu.VMEM((w.shape[1] // LANES, ROW_TILE, LANES), F32)],
        compiler_params=_cparams(1),
    )(h, w)


def _compress_one(c, pe, w1, w2p, ncp):
    half = CMP_STRIDE * HEAD_DIM
    hp = lax.Precision.HIGHEST
    h_lo = jnp.dot(c + pe[:, :half], w1[:half, :], preferred_element_type=F32, precision=hp)
    h_hi = jnp.dot(c + pe[:, half:], w1[half:, :], preferred_element_type=F32, precision=hp)
    pre = h_lo + pltpu.roll(h_hi, ncp - 1, 0)
    hid = pre * _sigmoid(pre)
    out = jnp.dot(hid, w2p, preferred_element_type=F32, precision=hp)
    row = lax.broadcasted_iota(jnp.int32, out.shape, 0)
    return jnp.where(row < ncp - 1, out, 0.0)


def _compress_body(ck_ref, cv_ref, pek_ref, w1k_ref, w2k_ref, pev_ref, w1v_ref, w2v_ref,
                   kc_ref, vc_ref):
    ncp = ck_ref.shape[0]
    kc = _compress_one(ck_ref[...], pek_ref[...], w1k_ref[...], w2k_ref[...], ncp)
    vc = _compress_one(cv_ref[...], pev_ref[...], w1v_ref[...], w2v_ref[...], ncp)
    kc_ref[...] = kc.astype(kc_ref.dtype)
    vc_ref[...] = vc.astype(vc_ref.dtype)


def _compress(ck, cv, pe_k, w1_k, w2_k, pe_v, w1_v, w2_v):
    b, g, ncp, cw = ck.shape
    flat = CMP_LEN * HEAD_DIM
    pad2 = lambda w: jnp.pad(w, ((0, 0), (0, LANES - HEAD_DIM)))
    tok = pl.BlockSpec((None, None, ncp, cw), lambda bi, gi: (bi, gi, 0, 0))
    full = lambda shp: pl.BlockSpec(shp, lambda bi, gi: (0,) * len(shp))
    out = pl.BlockSpec((None, None, ncp, LANES), lambda bi, gi: (bi, gi, 0, 0))
    return pl.pallas_call(
        _compress_body,
        grid=(b, g),
        in_specs=[tok, tok,
                  full((1, flat)), full((flat, HEAD_DIM)), full((HEAD_DIM, LANES)),
                  full((1, flat)), full((flat, HEAD_DIM)), full((HEAD_DIM, LANES))],
        out_specs=[out, out],
        out_shape=[jax.ShapeDtypeStruct((b, g, ncp, LANES), BF16)] * 2,
        compiler_params=_cparams(2),
    )(ck, cv,
      pe_k.reshape(1, flat), w1_k.reshape(flat, HEAD_DIM), pad2(w2_k),
      pe_v.reshape(1, flat), w1_v.reshape(flat, HEAD_DIM), pad2(w2_v))


def _nsa_body(n_top, q_ref, gl_ref, kc_ref, vc_ref, ks_ref, vs_ref, kw_ref, vw_ref,
              bw_ref, abc_ref, dmc_ref, tri_ref, slope_ref, ovt_ref, gsp_ref, o_ref,
              m_sc, acc_sc, qa_sc):
    i = pl.program_id(2)
    q0 = i * Q_BLOCK
    qb = q_ref[...]
    q = jnp.concatenate([qb[:, r * LANES:(r + 1) * LANES] for r in range(A_REP)], axis=0)
    n_sel = ovt_ref.shape[0]
    slope = slope_ref[...]

    valid_c = dmc_ref[...] <= q0.astype(F32)
    sc = jnp.where(valid_c, _dot_nt(q, kc_ref[...]) + abc_ref[...], NEG)
    mc = jnp.max(sc, axis=1, keepdims=True)
    pc = jnp.exp2(sc - mc)
    lc = jnp.sum(pc, axis=1, keepdims=True)
    pc = pc * jnp.where(mc > 0.5 * NEG, 1.0 / lc, 0.0)
    o_c = jnp.dot(pc.astype(BF16), vc_ref[...], preferred_element_type=F32)[:, :HEAD_DIM]

    psum = pc[0:Q_BLOCK]
    for r in range(1, A_REP):
        psum = psum + pc[r * Q_BLOCK:(r + 1) * Q_BLOCK]
    p1 = psum.astype(BF16)
    r1 = psum - p1.astype(F32)
    p2 = r1.astype(BF16)
    p3 = (r1 - p2.astype(F32)).astype(BF16)
    ovt = ovt_ref[...]
    imp_t = _dot_nt(ovt, p3) + _dot_nt(ovt, p2) + _dot_nt(ovt, p1)

    kb_w = pl.multiple_of(jnp.maximum(q0 - WINDOW, 0), Q_BLOCK)
    sw = _dot_nt(q, kw_ref[pl.ds(kb_w, WIN_KEYS), :]) + bw_ref[...]
    pw = jnp.exp2(sw - jnp.max(sw, axis=1, keepdims=True))
    accw = jnp.dot(pw.astype(BF16), vw_ref[pl.ds(kb_w, WIN_KEYS), :], preferred_element_type=F32)
    o_w = accw[:, :HEAD_DIM] / accw[:, HEAD_DIM:HEAD_DIM + 1]

    q0a = pl.multiple_of(q0, Q_BLOCK)
    kb_d = (q0 // SEL_TILE) * SEL_TILE
    c_d = slope * (kb_d - q0).astype(F32)
    s_d = _dot_nt(q, ks_ref[pl.ds(q0a, Q_BLOCK), LANES:2 * LANES]) + tri_ref[...]
    mx_d = jnp.broadcast_to(jnp.max(s_d, axis=1, keepdims=True), (QROWS, LANES))
    p_d = jnp.exp2(s_d - mx_d)
    acc_sc[...] = jnp.dot(p_d.astype(BF16), vs_ref[pl.ds(q0a, Q_BLOCK), :], preferred_element_type=F32)
    m_sc[...] = mx_d + c_d

    blk = lax.broadcasted_iota(jnp.int32, (n_sel, Q_BLOCK), 0)
    t_l = q0 + lax.broadcasted_iota(jnp.int32, (n_sel, Q_BLOCK), 1)
    cur = t_l >> 6
    causal_b = blk * SEL_BLOCK <= t_l
    forced = (blk == 0) | (blk == cur) | (blk == cur - 1)
    score = jnp.where(causal_b & jnp.logical_not(forced), imp_t, -1.0)
    blk_f = blk.astype(F32)
    sel = jnp.where(forced, 1.0, 0.0)
    for _ in range(n_top - 3):
        top = jnp.max(score, axis=0, keepdims=True)
        first = jnp.min(jnp.where(score == top, blk_f, float(n_sel)), axis=0, keepdims=True)
        pick = blk_f == first
        sel = jnp.where(pick, 1.0, sel)
        score = jnp.where(pick, -3e38, score)
    bias_t = jnp.where((sel > 0.0) & (blk * SEL_BLOCK < q0), 0.0, NEG)
    bias = bias_t.T.astype(BF16)
    qa_sc[...] = jnp.concatenate([jnp.concatenate([bias] * A_REP, axis=0), q], axis=1)

    def sel_tile(tile, m_old, acc_old):
        kb = pl.multiple_of(tile * SEL_TILE, SEL_TILE)
        s = _dot_nt(qa_sc[...], ks_ref[pl.ds(kb, SEL_TILE), :])
        c = slope * (kb - q0).astype(F32)
        mx = s[:, 0:LANES]
        for j in range(1, SEL_CHUNKS):
            mx = jnp.maximum(mx, s[:, j * LANES:(j + 1) * LANES])
        m_new = jnp.maximum(m_old, jnp.max(mx, axis=1, keepdims=True) + c)
        shift = m_new - c
        p = jnp.concatenate([jnp.exp2(s[:, j * LANES:(j + 1) * LANES] - shift).astype(BF16)
                             for j in range(SEL_CHUNKS)], axis=1)
        pv = jnp.dot(p, vs_ref[pl.ds(kb, SEL_TILE), :], preferred_element_type=F32)
        return m_new, jnp.exp2(m_old - m_new) * acc_old + pv

    def sel_pair(pair, carry):
        m1, acc1 = sel_tile(2 * pair, m_sc[...], acc_sc[...])
        m2, acc2 = sel_tile(2 * pair + 1, m1, acc1)
        m_sc[...] = m2
        acc_sc[...] = acc2
        return carry

    n_tiles = (q0 + SEL_TILE - 1) // SEL_TILE
    lax.fori_loop(0, n_tiles // 2, sel_pair, 0)

    @pl.when((n_tiles & 1) == 1)
    def _():
        m1, acc1 = sel_tile(n_tiles - 1, m_sc[...], acc_sc[...])
        m_sc[...] = m1
        acc_sc[...] = acc1

    acc = acc_sc[...]
    o_s = acc[:, :HEAD_DIM] / acc[:, HEAD_DIM:HEAD_DIM + 1]

    gate = _sigmoid(gl_ref[...])
    g_hi = gate.astype(BF16)
    g_lo = (gate - g_hi.astype(F32)).astype(BF16)
    spread = gsp_ref[...]
    g_rep = (jnp.dot(g_lo, spread, preferred_element_type=F32)
             + jnp.dot(g_hi, spread, preferred_element_type=F32))
    g_slot = lambda k: g_rep[:, k * LANES:k * LANES + HEAD_DIM]
    for r in range(A_REP):
        rs = slice(r * Q_BLOCK, (r + 1) * Q_BLOCK)
        o_ref[:, r * HEAD_DIM:(r + 1) * HEAD_DIM] = (
            g_slot(r) * o_c[rs] + g_slot(A_REP + r) * o_s[rs] + g_slot(2 * A_REP + r) * o_w[rs])


def _nsa(q_pad, glog, kc, vc, ks_ext, vs_ext, kw_ext, vw_ext, bw, abc, dmc, tri, slope_b, ovt, gsp):
    b, g, s = ks_ext.shape[:3]
    nq = s // Q_BLOCK
    ncp = kc.shape[2]
    n_top = min(SEL_TOP_N, s // SEL_BLOCK)
    assert n_top >= 3
    n_win_cases = bw.shape[1]
    per_bg = lambda shp: pl.BlockSpec((None, None) + shp, lambda bi, gi, i: (bi, gi, 0, 0))
    per_g = lambda shp: pl.BlockSpec((None,) + shp, lambda bi, gi, i: (gi, 0, 0))
    const = lambda shp: pl.BlockSpec(shp, lambda bi, gi, i: (0, 0))
    return pl.pallas_call(
        functools.partial(_nsa_body, n_top),
        grid=(b, g, nq),
        in_specs=[pl.BlockSpec((Q_BLOCK, A_REP * LANES), lambda bi, gi, i: (bi * nq + i, gi)),
                  pl.BlockSpec((Q_BLOCK, LANES), lambda bi, gi, i: (bi * nq + i, gi)),
                  per_bg((ncp, LANES)), per_bg((ncp, LANES)),
                  per_bg((s, ks_ext.shape[3])), per_bg((s, LANES)),
                  per_bg((s, LANES)), per_bg((s, LANES)),
                  pl.BlockSpec((None, None, QROWS, WIN_KEYS),
                               lambda bi, gi, i: (gi, jnp.minimum(i, n_win_cases - 1), 0, 0)),
                  per_g((QROWS, ncp)), const((QROWS, ncp)), const((QROWS, Q_BLOCK)),
                  per_g((QROWS, LANES)), const(ovt.shape), const(gsp.shape)],
        out_specs=pl.BlockSpec((None, Q_BLOCK, A_REP * HEAD_DIM), lambda bi, gi, i: (bi, i, gi)),
        out_shape=jax.ShapeDtypeStruct((b, s, A_WIDTH), F32),
        scratch_shapes=[pltpu.VMEM((QROWS, LANES), F32), pltpu.VMEM((QROWS, LANES), F32),
                        pltpu.VMEM((QROWS, 2 * LANES), BF16)],
        compiler_params=_cparams(3),
    )(q_pad, glog, kc, vc, ks_ext, vs_ext, kw_ext, vw_ext, bw, abc, dmc, tri, slope_b, ovt, gsp)


def _gelu(x):
    return 0.5 * x * (1.0 + jnp.tanh(np.float32(np.sqrt(2.0 / np.pi)) * (x + 0.044715 * (x * x * x))))


def _sgu_body(uv_ref, zb_ref, lng_ref, lnb_ref, w_ref, bs_ref, o_ref):
    bw = zb_ref.shape[1]
    gd = bw // B_GROUPS
    u = _gelu(uv_ref[:, :bw])
    v = _gelu(uv_ref[:, bw:])
    mu = jnp.mean(v, axis=-1, keepdims=True)
    vc = v - mu
    vn = vc * lax.rsqrt(jnp.mean(vc * vc, axis=-1, keepdims=True) + EPS)
    vn = (vn * lng_ref[...] + lnb_ref[...]).astype(BF16)
    zb = zb_ref[...]
    gate = u * (zb * _sigmoid(zb))
    ti = lax.broadcasted_iota(jnp.int32, (B_CHUNK, B_CHUNK), 0)
    si = lax.broadcasted_iota(jnp.int32, (B_CHUNK, B_CHUNK), 1)
    for gi in range(B_GROUPS):
        wg = jnp.where(si <= ti, w_ref[gi], 0.0).astype(BF16)
        bcol = bs_ref[:, gi:gi + 1]
        for c in range(uv_ref.shape[0] // B_CHUNK):
            rs = slice(c * B_CHUNK, (c + 1) * B_CHUNK)
            cs = slice(gi * gd, (gi + 1) * gd)
            sv = jnp.dot(wg, vn[rs, cs], preferred_element_type=F32) + bcol
            o_ref[rs, cs] = (gate[rs, cs] * sv).astype(o_ref.dtype)


def _sgu(uv, zb, ln_g, ln_b, w_s, b_s):
    m, bw2 = uv.shape
    bw = bw2 // 2
    return pl.pallas_call(
        _sgu_body,
        grid=(m // ROW_TILE,),
        in_specs=[_row_tile_spec(bw2), _row_tile_spec(bw),
                  _resident_spec((1, bw)), _resident_spec((1, bw)),
                  _resident_spec(w_s.shape), _resident_spec((B_CHUNK, B_GROUPS))],
        out_specs=_row_tile_spec(bw),
        out_shape=jax.ShapeDtypeStruct((m, bw), BF16),
        compiler_params=_cparams(1),
    )(uv, zb, ln_g.reshape(1, bw), ln_b.reshape(1, bw), w_s, b_s.T)


def _merge_body(oa_ref, za_ref, ob_ref, ml_ref, wa_ref, wb_ref, o_ref):
    d = o_ref.shape[1]
    za = za_ref[...]
    oa = (oa_ref[...] * (za * _sigmoid(za))).astype(BF16)
    ua = jnp.dot(oa, wa_ref[...], preferred_element_type=F32)
    ub = jnp.dot(ob_ref[...], wb_ref[...], preferred_element_type=F32)
    o_ref[...] = (_sigmoid(ml_ref[:, :d]) * ua + _sigmoid(ml_ref[:, d:]) * ub).astype(o_ref.dtype)


def _merge(oa, za, ob, ml, w_up_a, w_up_b):
    m, aw = oa.shape
    bw = ob.shape[1]
    d = w_up_a.shape[1]
    tm = ROW_TILE // 2
    row = lambda w: pl.BlockSpec((tm, w), lambda i: (i, 0))
    return pl.pallas_call(
        _merge_body,
        grid=(m // tm,),
        in_specs=[row(aw), row(aw), row(bw), row(2 * d),
                  _resident_spec((aw, d)), _resident_spec((bw, d))],
        out_specs=row(d),
        out_shape=jax.ShapeDtypeStruct((m, d), BF16),
        compiler_params=_cparams(1),
    )(oa, za, ob, ml, w_up_a, w_up_b)


def _out_body(x_ref, mg_ref, p_ref, wo_ref, wg_ref, wp_ref, fg_ref, o_ref):
    x1 = x_ref[...] + jnp.dot(mg_ref[...], wo_ref[...], preferred_element_type=F32)
    gl = jnp.dot(x1.astype(BF16), wg_ref[...], preferred_element_type=F32)
    ple = jnp.dot(p_ref[...].astype(BF16), wp_ref[...], preferred_element_type=F32)
    x2 = x1 + _sigmoid(gl) * ple
    y = x2 * lax.rsqrt(jnp.mean(x2 * x2, axis=-1, keepdims=True) + EPS)
    o_ref[...] = y * fg_ref[...]


def _out(x2d, merged, p2d, w_out, w_gate, w_ple, final_g):
    m, d = x2d.shape
    pd = p2d.shape[1]
    tm = ROW_TILE // 2
    row = lambda w: pl.BlockSpec((tm, w), lambda i: (i, 0))
    return pl.pallas_call(
        _out_body,
        grid=(m // tm,),
        in_specs=[row(d), row(d), row(pd),
                  _resident_spec((d, d)), _resident_spec((d, d)), _resident_spec((pd, d)),
                  _resident_spec((1, d))],
        out_specs=row(d),
        out_shape=jax.ShapeDtypeStruct((m, d), F32),
        compiler_params=_cparams(1),
    )(x2d, merged, p2d, w_out, w_gate, w_ple, final_g.reshape(1, d))


def _bf16_parts(x, n):
    parts, rest = [], jnp.asarray(x, F32)
    for _ in range(n):
        piece = rest.astype(BF16)
        parts.append(piece)
        rest = rest - piece.astype(F32)
    return parts


def _attention_constants(s, ncp):
    hh = np.arange(1, A_HEADS + 1, dtype=np.float32)
    slopes = np.power(np.float32(2.0), -8.0 * hh / A_HEADS).astype(np.float32)
    slopes2 = (slopes * np.float32(LOG2E)).astype(np.float32)
    slope_col = jnp.asarray(np.repeat(slopes2.reshape(A_KV_GROUPS, A_REP), Q_BLOCK, axis=1)[:, :, None])
    slope_b = jnp.broadcast_to(slope_col, (A_KV_GROUPS, QROWS, LANES))
    qi = (jnp.arange(QROWS, dtype=jnp.int32) % Q_BLOCK).astype(F32)
    dm = jnp.arange(WIN_KEYS, dtype=F32)[None, :] - qi[:, None]
    offs = jnp.arange(WINDOW // Q_BLOCK + 1, dtype=F32)[:, None, None] * Q_BLOCK
    vis = (dm[None] <= offs) & (dm[None] > offs - WINDOW)
    bw = jnp.where(vis[None], (slope_col * dm[None])[:, None], NEG)
    tri = jnp.where(dm[:, :Q_BLOCK] <= 0, 0.0, NEG)
    c_end = jnp.arange(ncp, dtype=F32) * CMP_STRIDE + (CMP_LEN - 1)
    dmc = c_end[None, :] - qi[:, None]
    abc = slope_col * dmc[None]
    ci = np.arange(ncp)[None, :]
    sj = np.arange(LANES)[:, None]
    ovt = ((CMP_STRIDE * ci < SEL_BLOCK * sj + SEL_BLOCK) &
           (CMP_STRIDE * ci + CMP_LEN > SEL_BLOCK * sj) &
           (ci < ncp - 1)).astype(np.float32)
    parts = _bf16_parts(slopes2, N_SLOPE_PARTS)
    cols = jnp.stack([pp * POS_LO for pp in parts] + parts, axis=1)
    q_cols = jnp.pad(cols, ((0, 0), (0, HEAD_DIM - cols.shape[1]))).reshape(1, A_HEADS * HEAD_DIM)
    n_gates = 3 * A_REP
    gsp = np.zeros((LANES, n_gates * LANES), np.float32)
    for k in range(n_gates):
        gsp[k, k * LANES:k * LANES + HEAD_DIM] = 1.0
    return bw, abc, dmc, tri, slope_b, jnp.asarray(ovt, dtype=BF16), q_cols, jnp.asarray(gsp, dtype=BF16)


def kernel(x, p, norm_g, w_in, cmp_pe_k, cmp_w1_k, cmp_w2_k, cmp_pe_v, cmp_w1_v, cmp_w2_v, ln_v_g, ln_v_b, sgu_w, sgu_b, w_up_a, w_up_b, w_out, w_ple, w_ple_gate, final_g):
    b, s, d = x.shape
    assert p.shape[0] == 1, "the final norm is fused into the (single) layer's output kernel"
    m = b * s
    g, r = A_KV_GROUPS, A_REP
    nq = s // Q_BLOCK
    ncp = s // CMP_STRIDE
    b_width = w_up_b.shape[1]
    x2d = x.reshape(m, d)
    w_in0 = w_in[0]

    o_kv = A_WIDTH
    o_gl = o_kv + 6 * A_KV_WIDTH
    o_za = o_gl + 3 * A_HEADS
    o_uv = o_za + A_WIDTH
    o_zb = o_uv + 2 * b_width
    o_ml = o_zb + b_width
    wcast = lambda lo, hi: w_in0[:, lo:hi].astype(BF16)
    w_q = (w_in0[:, :o_kv] * np.float32(HEAD_DIM ** -0.5 * LOG2E)).astype(BF16)
    w_gl = w_in0[:, o_gl:o_za].reshape(d, 3, g, r).transpose(0, 2, 1, 3).reshape(d, g, 3 * r)
    w_gl = jnp.pad(w_gl, ((0, 0), (0, 0), (0, LANES - 3 * r))).reshape(d, g * LANES).astype(BF16)

    bw, abc, dmc, tri, slope_b, ovt, q_cols, gsp = _attention_constants(s, ncp)

    h = _rmsnorm(x2d, norm_g[0])
    q_pad = _project_q(h, w_q, q_cols)
    ck, cv = _project_kvc(h, wcast(o_kv, o_kv + 2 * A_KV_WIDTH), b, s)
    ks_ext, vs_ext, kw_ext, vw_ext = _project_kvsw(h, wcast(o_kv + 2 * A_KV_WIDTH, o_gl), b, s)
    glog = _project(h, w_gl, F32)
    za = _project(h, wcast(o_za, o_uv), F32)
    uv = _project(h, wcast(o_uv, o_zb), F32)
    zb = _project(h, wcast(o_zb, o_ml), F32)
    ml = _project(h, wcast(o_ml, w_in0.shape[1]), F32)

    kc, vc = _compress(ck, cv, cmp_pe_k[0], cmp_w1_k[0], cmp_w2_k[0], cmp_pe_v[0], cmp_w1_v[0], cmp_w2_v[0])
    o_att = _nsa(q_pad, glog, kc, vc, ks_ext, vs_ext, kw_ext, vw_ext, bw, abc, dmc, tri, slope_b, ovt, gsp)

    ob = _sgu(uv, zb, ln_v_g[0], ln_v_b[0], sgu_w[0], sgu_b[0])
    merged = _merge(o_att.reshape(m, A_WIDTH), za, ob, ml, w_up_a[0].astype(BF16), w_up_b[0].astype(BF16))
    out = _out(x2d, merged, p[0].reshape(m, -1), w_out[0].astype(BF16), w_ple_gate[0].astype(BF16),
               w_ple[0].astype(BF16), final_g)
    return out.reshape(b, s, d)
```

```python
import functools

import numpy as np
import jax
import jax.numpy as jnp
from jax import lax
from jax.experimental import pallas as pl
from jax.experimental.pallas import tpu as pltpu

A_HEADS = 16
A_KV_GROUPS = 4
A_REP = A_HEADS // A_KV_GROUPS
HEAD_DIM = 64
A_WIDTH = A_HEADS * HEAD_DIM
A_KV_WIDTH = A_KV_GROUPS * HEAD_DIM
CMP_LEN = 32
CMP_STRIDE = 16
SEL_BLOCK = 64
SEL_TOP_N = 16
WINDOW = 512
Q_BLOCK = 128
FORCE_BONUS = 1000.0
B_GROUPS = 8
B_CHUNK = 128
NEG = -1e30
EPS = 1e-6
LOG2E = float(np.log2(np.e))

LANES = 128
QROWS = A_REP * Q_BLOCK
NSA_BLOCKS = 2
SEL_TILE = 1024
SEL_CHUNKS = SEL_TILE // LANES
WIN_KEYS = WINDOW + Q_BLOCK
ROW_TILE = 512
VMEM_LIMIT = 56 * 1024 * 1024
POS_SHIFT = 5
POS_LO = 1 << POS_SHIFT
N_SLOPE_PARTS = 3

F32 = jnp.float32
BF16 = jnp.bfloat16


def _cparams(n_axes):
    return pltpu.CompilerParams(dimension_semantics=("arbitrary",) * n_axes,
                                vmem_limit_bytes=VMEM_LIMIT)


def _sigmoid(x):
    return 1.0 / (1.0 + jnp.exp(-x))


def _dot_nt(a, b):
    return lax.dot_general(a, b, (((1,), (1,)), ((), ())), preferred_element_type=F32)


def _row_tile_spec(width):
    return pl.BlockSpec((ROW_TILE, width), lambda i: (i, 0))


def _resident_spec(shape):
    return pl.BlockSpec(shape, lambda i: (0,) * len(shape))


def _norm_body(x_ref, g_ref, h_ref):
    x = x_ref[...]
    y = x * lax.rsqrt(jnp.mean(x * x, axis=-1, keepdims=True) + EPS)
    h_ref[...] = (y * g_ref[...]).astype(h_ref.dtype)


def _rmsnorm(x2d, g):
    m, d = x2d.shape
    return pl.pallas_call(
        _norm_body,
        grid=(m // ROW_TILE,),
        in_specs=[_row_tile_spec(d), _resident_spec((1, d))],
        out_specs=_row_tile_spec(d),
        out_shape=jax.ShapeDtypeStruct((m, d), BF16),
        compiler_params=_cparams(1),
    )(x2d, g.reshape(1, d))


def _mm_body(h_ref, w_ref, o_ref):
    o_ref[...] = _dot_nt(h_ref[...], w_ref[...]).astype(o_ref.dtype)


def _project(h, w, out_dtype):
    m, k = h.shape
    n = w.shape[0]
    tn = min(n, 2048)
    return pl.pallas_call(
        _mm_body,
        grid=(n // tn, m // ROW_TILE),
        in_specs=[pl.BlockSpec((ROW_TILE, k), lambda j, i: (i, 0)),
                  pl.BlockSpec((tn, k), lambda j, i: (j, 0))],
        out_specs=pl.BlockSpec((ROW_TILE, tn), lambda j, i: (i, j)),
        out_shape=jax.ShapeDtypeStruct((m, n), out_dtype),
        compiler_params=_cparams(2),
    )(h, w)


def _q_body(h_ref, w_ref, qc_ref, o_ref):
    res = _dot_nt(h_ref[...], w_ref[...])
    rows = res.shape[0]
    for hd in range(A_HEADS):
        cs = slice(hd * HEAD_DIM, (hd + 1) * HEAD_DIM)
        cst = jnp.broadcast_to(qc_ref[:, cs], (rows, HEAD_DIM))
        o_ref[:, hd * LANES:(hd + 1) * LANES] = jnp.concatenate([res[:, cs].astype(BF16), cst], axis=1)


def _project_q(h, w_q, q_cols):
    m, k = h.shape
    return pl.pallas_call(
        _q_body,
        grid=(m // ROW_TILE,),
        in_specs=[_row_tile_spec(k), _resident_spec(w_q.shape), _resident_spec(q_cols.shape)],
        out_specs=_row_tile_spec(A_HEADS * LANES),
        out_shape=jax.ShapeDtypeStruct((m, A_HEADS * LANES), BF16),
        compiler_params=_cparams(1),
    )(h, w_q, q_cols)


def _kvsw_body(seq, h_ref, w_ref, ks_ref, vs_ref, kw_ref, vw_ref):
    res = _dot_nt(h_ref[...], w_ref[...])
    rows = res.shape[0]
    tok0 = (pl.program_id(0) * rows) % seq
    pos = tok0 + lax.broadcasted_iota(jnp.int32, (rows, LANES), 0)
    lane = lax.broadcasted_iota(jnp.int32, (rows, LANES), 1)
    onehot = jnp.where((pos >> 6) == lane, 1.0, 0.0).astype(BF16)
    off = (tok0 + lax.broadcasted_iota(jnp.int32, (rows, HEAD_DIM), 0)) & (SEL_TILE - 1)
    lane_h = lax.broadcasted_iota(jnp.int32, (rows, HEAD_DIM), 1)
    pos_cols = jnp.where(lane_h < N_SLOPE_PARTS, off >> POS_SHIFT,
                         jnp.where(lane_h < 2 * N_SLOPE_PARTS, off & (POS_LO - 1), 0)).astype(F32).astype(BF16)
    ones_col = jnp.where(lane_h == 0, 1.0, 0.0).astype(BF16)
    zeros = jnp.zeros((rows, HEAD_DIM), BF16)
    piece = lambda j: res[:, j * HEAD_DIM:(j + 1) * HEAD_DIM].astype(BF16)
    for g in range(A_KV_GROUPS):
        ks_ref[g, :, 0:LANES] = onehot
        ks_ref[g, :, LANES:2 * LANES] = jnp.concatenate([piece(g), pos_cols], axis=1)
        vs_ref[g] = jnp.concatenate([piece(A_KV_GROUPS + g), ones_col], axis=1)
        kw_ref[g] = jnp.concatenate([piece(2 * A_KV_GROUPS + g), zeros], axis=1)
        vw_ref[g] = jnp.concatenate([piece(3 * A_KV_GROUPS + g), ones_col], axis=1)


def _project_kvsw(h, w, b, s):
    m, k = h.shape
    tps = s // ROW_TILE
    assert s // SEL_BLOCK <= LANES, "one-hot selection columns must fit one lane tile"
    spec = lambda width: pl.BlockSpec((None, A_KV_GROUPS, ROW_TILE, width),
                                      lambda i: (i // tps, 0, i % tps, 0))
    shape = lambda width: jax.ShapeDtypeStruct((b, A_KV_GROUPS, s, width), BF16)
    return pl.pallas_call(
        lambda *refs: _kvsw_body(s, *refs),
        grid=(m // ROW_TILE,),
        in_specs=[_row_tile_spec(k), _resident_spec(w.shape)],
        out_specs=[spec(2 * LANES), spec(LANES), spec(LANES), spec(LANES)],
        out_shape=[shape(2 * LANES), shape(LANES), shape(LANES), shape(LANES)],
        compiler_params=_cparams(1),
    )(h, w)


def _kvc_body(h_ref, w_ref, ck_ref, cv_ref, res_sc):
    res = _dot_nt(h_ref[...], w_ref[...])
    n_slabs = res_sc.shape[0]
    for c in range(n_slabs):
        res_sc[c] = res[:, c * LANES:(c + 1) * LANES]
    nrow = res_sc.shape[1] // CMP_STRIDE
    heads_per_slab = LANES // HEAD_DIM
    for c in range(n_slabs):
        out_ref = ck_ref if c < n_slabs // 2 else cv_ref
        g0 = (c % (n_slabs // 2)) * heads_per_slab
        for l in range(CMP_STRIDE):
            slab = res_sc[c, pl.ds(l, nrow, stride=CMP_STRIDE), :]
            for j in range(heads_per_slab):
                out_ref[g0 + j, :, l * HEAD_DIM:(l + 1) * HEAD_DIM] = slab[:, j * HEAD_DIM:(j + 1) * HEAD_DIM]


def _project_kvc(h, w, b, s):
    m, k = h.shape
    tps = s // ROW_TILE
    nrow = ROW_TILE // CMP_STRIDE
    width = CMP_STRIDE * HEAD_DIM
    spec = pl.BlockSpec((None, A_KV_GROUPS, nrow, width), lambda i: (i // tps, 0, i % tps, 0))
    shape = jax.ShapeDtypeStruct((b, A_KV_GROUPS, s // CMP_STRIDE, width), F32)
    return pl.pallas_call(
        _kvc_body,
        grid=(m // ROW_TILE,),
        in_specs=[_row_tile_spec(k), _resident_spec(w.shape)],
        out_specs=[spec, spec],
        out_shape=[shape, shape],
        scratch_shapes=[pltpu.VMEM((w.shape[0] // LANES, ROW_TILE, LANES), F32)],
        compiler_params=_cparams(1),
    )(h, w)


def _compress_one(c, pe, w1, w2p, ncp):
    half = CMP_STRIDE * HEAD_DIM
    hp = lax.Precision.HIGHEST
    h_lo = jnp.dot(c + pe[:, :half], w1[:half, :], preferred_element_type=F32, precision=hp)
    h_hi = jnp.dot(c + pe[:, half:], w1[half:, :], preferred_element_type=F32, precision=hp)
    pre = h_lo + pltpu.roll(h_hi, ncp - 1, 0)
    hid = pre * _sigmoid(pre)
    out = jnp.dot(hid, w2p, preferred_element_type=F32, precision=hp)
    row = lax.broadcasted_iota(jnp.int32, out.shape, 0)
    return jnp.where(row < ncp - 1, out, 0.0)


def _compress_body(ck_ref, cv_ref, pek_ref, w1k_ref, w2k_ref, pev_ref, w1v_ref, w2v_ref,
                   kc_ref, vc_ref):
    ncp = ck_ref.shape[0]
    kc = _compress_one(ck_ref[...], pek_ref[...], w1k_ref[...], w2k_ref[...], ncp)
    vc = _compress_one(cv_ref[...], pev_ref[...], w1v_ref[...], w2v_ref[...], ncp)
    kc_ref[...] = kc.astype(kc_ref.dtype)
    vc_ref[...] = vc.astype(vc_ref.dtype)


def _compress(ck, cv, pe_k, w1_k, w2_k, pe_v, w1_v, w2_v):
    b, g, ncp, cw = ck.shape
    flat = CMP_LEN * HEAD_DIM
    pad2 = lambda w: jnp.pad(w, ((0, 0), (0, LANES - HEAD_DIM)))
    tok = pl.BlockSpec((None, None, ncp, cw), lambda bi, gi: (bi, gi, 0, 0))
    full = lambda shp: pl.BlockSpec(shp, lambda bi, gi: (0,) * len(shp))
    out = pl.BlockSpec((None, None, ncp, LANES), lambda bi, gi: (bi, gi, 0, 0))
    return pl.pallas_call(
        _compress_body,
        grid=(b, g),
        in_specs=[tok, tok,
                  full((1, flat)), full((flat, HEAD_DIM)), full((HEAD_DIM, LANES)),
                  full((1, flat)), full((flat, HEAD_DIM)), full((HEAD_DIM, LANES))],
        out_specs=[out, out],
        out_shape=[jax.ShapeDtypeStruct((b, g, ncp, LANES), BF16)] * 2,
        compiler_params=_cparams(2),
    )(ck, cv,
      pe_k.reshape(1, flat), w1_k.reshape(flat, HEAD_DIM), pad2(w2_k),
      pe_v.reshape(1, flat), w1_v.reshape(flat, HEAD_DIM), pad2(w2_v))


def _nsa_front(n_top, first, qs, kc_ref, vc_ref, ks_ref, vs_ref, kw_ref, vw_ref,
               bw_refs, abc_ref, dmc_ref, tri_ref, slope, ovt_ref, m_sc, acc_sc, qa_sc, used_ref):
    nb = len(qs)
    q0s = [(first + k) * Q_BLOCK for k in range(nb)]
    n_sel = ovt_ref.shape[0]
    rows_of = lambda k: slice(k * QROWS, (k + 1) * QROWS)
    stack = lambda parts: jnp.concatenate(parts, axis=0)

    s_all = _dot_nt(stack(qs), kc_ref[...])
    dmc = dmc_ref[...]
    abc = abc_ref[...]
    sc = stack([jnp.where(dmc <= q0s[k].astype(F32), s_all[rows_of(k)] + abc, NEG) for k in range(nb)])
    mc = jnp.max(sc, axis=1, keepdims=True)
    pc = jnp.exp2(sc - mc)
    lc = jnp.sum(pc, axis=1, keepdims=True)
    pc = pc * jnp.where(mc > 0.5 * NEG, 1.0 / lc, 0.0)
    o_c = jnp.dot(pc.astype(BF16), vc_ref[...], preferred_element_type=F32)[:, :HEAD_DIM]

    head_sum = lambda k: functools.reduce(
        jnp.add, [pc[k * QROWS + r * Q_BLOCK:k * QROWS + (r + 1) * Q_BLOCK] for r in range(A_REP)])
    psum = stack([head_sum(k) for k in range(nb)])
    p1 = psum.astype(BF16)
    r1 = psum - p1.astype(F32)
    p2 = r1.astype(BF16)
    p3 = (r1 - p2.astype(F32)).astype(BF16)
    ovt = ovt_ref[...]
    imp_t = _dot_nt(ovt, p3) + _dot_nt(ovt, p2) + _dot_nt(ovt, p1)

    kb_w = [pl.multiple_of(jnp.maximum(q0 - WINDOW, 0), Q_BLOCK) for q0 in q0s]
    sw = stack([_dot_nt(qs[k], kw_ref[pl.ds(kb_w[k], WIN_KEYS), :]) + bw_refs[k][...] for k in range(nb)])
    pw = jnp.exp2(sw - jnp.max(sw, axis=1, keepdims=True)).astype(BF16)
    accw = stack([jnp.dot(pw[rows_of(k)], vw_ref[pl.ds(kb_w[k], WIN_KEYS), :], preferred_element_type=F32)
                  for k in range(nb)])
    o_w = accw[:, :HEAD_DIM] / accw[:, HEAD_DIM:HEAD_DIM + 1]

    q0a = [pl.multiple_of(q0, Q_BLOCK) for q0 in q0s]
    tri = tri_ref[...]
    s_d = stack([_dot_nt(qs[k], ks_ref[pl.ds(q0a[k], Q_BLOCK), LANES:2 * LANES]) + tri for k in range(nb)])
    mx_d = jnp.broadcast_to(jnp.max(s_d, axis=1, keepdims=True), (nb * QROWS, LANES))
    p_d = jnp.exp2(s_d - mx_d).astype(BF16)
    for k in range(nb):
        acc_sc[k] = jnp.dot(p_d[rows_of(k)], vs_ref[pl.ds(q0a[k], Q_BLOCK), :], preferred_element_type=F32)
        kb_d = (q0s[k] // SEL_TILE) * SEL_TILE
        m_sc[k] = mx_d[rows_of(k)] + slope * (kb_d - q0s[k]).astype(F32)

    blk = lax.broadcasted_iota(jnp.int32, (n_sel, nb * Q_BLOCK), 0)
    lane_q = lax.broadcasted_iota(jnp.int32, (n_sel, nb * Q_BLOCK), 1)
    t_l = q0s[0] + lane_q
    cur = t_l >> 6
    causal_b = blk * SEL_BLOCK <= t_l
    forced = (blk == 0) | (blk == cur) | (blk == cur - 1)
    score = jnp.where(causal_b & jnp.logical_not(forced), imp_t, -1.0)
    blk_f = blk.astype(F32)
    sel = jnp.where(forced, 1.0, 0.0)
    for _ in range(n_top - 3):
        top = jnp.max(score, axis=0, keepdims=True)
        first = jnp.min(jnp.where(score == top, blk_f, float(n_sel)), axis=0, keepdims=True)
        pick = blk_f == first
        sel = jnp.where(pick, 1.0, sel)
        score = jnp.where(pick, -3e38, score)
    q0_l = t_l & ~(Q_BLOCK - 1)
    in_loop = (sel > 0.0) & (blk * SEL_BLOCK < q0_l)
    bias_t = jnp.where(in_loop, 0.0, NEG)
    used = jnp.where(in_loop, 1.0, 0.0)
    blocks_per_tile = SEL_TILE // SEL_BLOCK
    for t in range(used_ref.shape[0]):
        used_ref[t] = jnp.max(used[t * blocks_per_tile:(t + 1) * blocks_per_tile]).astype(jnp.int32)
    bias = bias_t.T.astype(BF16)
    for k in range(nb):
        bias_k = bias[k * Q_BLOCK:(k + 1) * Q_BLOCK]
        qa_sc[k] = jnp.concatenate([jnp.concatenate([bias_k] * A_REP, axis=0), qs[k]], axis=1)
    return o_c, o_w


def _nsa_body(n_top, q_ref, gl_ref, kc_ref, vc_ref, ks_ref, vs_ref, kw_ref, vw_ref, *rest):
    bw_refs = rest[:NSA_BLOCKS]
    (abc_ref, dmc_ref, tri_ref, slope_ref, ovt_ref, gsp_ref, o_ref,
     m_sc, acc_sc, qa_sc, used_ref) = rest[NSA_BLOCKS:]
    first = pl.program_id(2) * NSA_BLOCKS
    slope = slope_ref[...]
    qb = q_ref[...]
    qs = [jnp.concatenate([qb[k * Q_BLOCK:(k + 1) * Q_BLOCK, r * LANES:(r + 1) * LANES]
                           for r in range(A_REP)], axis=0) for k in range(NSA_BLOCKS)]
    o_c, o_w = _nsa_front(n_top, first, qs, kc_ref, vc_ref, ks_ref, vs_ref, kw_ref, vw_ref,
                          bw_refs, abc_ref, dmc_ref, tri_ref, slope, ovt_ref, m_sc, acc_sc, qa_sc, used_ref)

    def sel_tile(tile, carry):
        @pl.when(used_ref[tile] > 0)
        def _():
            kb = pl.multiple_of(tile * SEL_TILE, SEL_TILE)
            for k in range(NSA_BLOCKS):
                s = _dot_nt(qa_sc[k], ks_ref[pl.ds(kb, SEL_TILE), :])
                c = slope * (kb - (first + k) * Q_BLOCK).astype(F32)
                mx = s[:, 0:LANES]
                for j in range(1, SEL_CHUNKS):
                    mx = jnp.maximum(mx, s[:, j * LANES:(j + 1) * LANES])
                m_old = m_sc[k]
                m_new = jnp.maximum(m_old, jnp.max(mx, axis=1, keepdims=True) + c)
                shift = m_new - c
                p = jnp.concatenate([jnp.exp2(s[:, j * LANES:(j + 1) * LANES] - shift).astype(BF16)
                                     for j in range(SEL_CHUNKS)], axis=1)
                pv = jnp.dot(p, vs_ref[pl.ds(kb, SEL_TILE), :], preferred_element_type=F32)
                acc_sc[k] = jnp.exp2(m_old - m_new) * acc_sc[k] + pv
                m_sc[k] = m_new
        return carry

    q0_last = (first + NSA_BLOCKS - 1) * Q_BLOCK
    lax.fori_loop(0, (q0_last + SEL_TILE - 1) // SEL_TILE, sel_tile, 0)

    gate = _sigmoid(gl_ref[...])
    g_hi = gate.astype(BF16)
    g_lo = (gate - g_hi.astype(F32)).astype(BF16)
    spread = gsp_ref[...]
    g_rep = (jnp.dot(g_lo, spread, preferred_element_type=F32)
             + jnp.dot(g_hi, spread, preferred_element_type=F32))
    for k in range(NSA_BLOCKS):
        acc = acc_sc[k]
        o_s = acc[:, :HEAD_DIM] / acc[:, HEAD_DIM:HEAD_DIM + 1]
        q_rows = slice(k * Q_BLOCK, (k + 1) * Q_BLOCK)
        g_slot = lambda n: g_rep[q_rows, n * LANES:n * LANES + HEAD_DIM]
        for r in range(A_REP):
            rs = slice(r * Q_BLOCK, (r + 1) * Q_BLOCK)
            rk = slice(k * QROWS + r * Q_BLOCK, k * QROWS + (r + 1) * Q_BLOCK)
            o_ref[q_rows, r * HEAD_DIM:(r + 1) * HEAD_DIM] = (
                g_slot(r) * o_c[rk] + g_slot(A_REP + r) * o_s[rs] + g_slot(2 * A_REP + r) * o_w[rk])


def _nsa(q_pad, glog, kc, vc, ks_ext, vs_ext, kw_ext, vw_ext, bw, abc, dmc, tri, slope_b, ovt, gsp):
    b, g, s = ks_ext.shape[:3]
    nq = s // Q_BLOCK
    steps = nq // NSA_BLOCKS
    rows = NSA_BLOCKS * Q_BLOCK
    ncp = kc.shape[2]
    n_top = min(SEL_TOP_N, s // SEL_BLOCK)
    assert n_top >= 3
    n_win_cases = bw.shape[1]
    per_bg = lambda shp: pl.BlockSpec((None, None) + shp, lambda bi, gi, i: (bi, gi, 0, 0))
    per_g = lambda shp: pl.BlockSpec((None,) + shp, lambda bi, gi, i: (gi, 0, 0))
    const = lambda shp: pl.BlockSpec(shp, lambda bi, gi, i: (0, 0))
    win_case = lambda k: pl.BlockSpec(
        (None, None, QROWS, WIN_KEYS),
        lambda bi, gi, i: (gi, jnp.minimum(i * NSA_BLOCKS + k, n_win_cases - 1), 0, 0))
    return pl.pallas_call(
        functools.partial(_nsa_body, n_top),
        grid=(b, g, steps),
        in_specs=[pl.BlockSpec((rows, A_REP * LANES), lambda bi, gi, i: (bi * steps + i, gi)),
                  pl.BlockSpec((rows, LANES), lambda bi, gi, i: (bi * steps + i, gi)),
                  per_bg((ncp, LANES)), per_bg((ncp, LANES)),
                  per_bg((s, ks_ext.shape[3])), per_bg((s, LANES)),
                  per_bg((s, LANES)), per_bg((s, LANES))]
                 + [win_case(k) for k in range(NSA_BLOCKS)]
                 + [per_g((QROWS, ncp)), const((QROWS, ncp)), const((QROWS, Q_BLOCK)),
                    per_g((QROWS, LANES)), const(ovt.shape), const(gsp.shape)],
        out_specs=pl.BlockSpec((None, rows, A_REP * HEAD_DIM), lambda bi, gi, i: (bi, i, gi)),
        out_shape=jax.ShapeDtypeStruct((b, s, A_WIDTH), F32),
        scratch_shapes=[pltpu.VMEM((NSA_BLOCKS, QROWS, LANES), F32),
                        pltpu.VMEM((NSA_BLOCKS, QROWS, LANES), F32),
                        pltpu.VMEM((NSA_BLOCKS, QROWS, 2 * LANES), BF16),
                        pltpu.SMEM((s // SEL_TILE,), jnp.int32)],
        compiler_params=_cparams(3),
    )(q_pad, glog, kc, vc, ks_ext, vs_ext, kw_ext, vw_ext, *([bw] * NSA_BLOCKS),
      abc, dmc, tri, slope_b, ovt, gsp)


def _gelu(x):
    return 0.5 * x * (1.0 + jnp.tanh(np.float32(np.sqrt(2.0 / np.pi)) * (x + 0.044715 * (x * x * x))))


def _sgu_body(uv_ref, zb_ref, lng_ref, lnb_ref, w_ref, bs_ref, o_ref):
    bw = zb_ref.shape[1]
    gd = bw // B_GROUPS
    u = _gelu(uv_ref[:, :bw])
    v = _gelu(uv_ref[:, bw:])
    mu = jnp.mean(v, axis=-1, keepdims=True)
    vc = v - mu
    vn = vc * lax.rsqrt(jnp.mean(vc * vc, axis=-1, keepdims=True) + EPS)
    vn = (vn * lng_ref[...] + lnb_ref[...]).astype(BF16)
    zb = zb_ref[...]
    gate = u * (zb * _sigmoid(zb))
    ti = lax.broadcasted_iota(jnp.int32, (B_CHUNK, B_CHUNK), 0)
    si = lax.broadcasted_iota(jnp.int32, (B_CHUNK, B_CHUNK), 1)
    for gi in range(B_GROUPS):
        wg = jnp.where(si <= ti, w_ref[gi], 0.0).astype(BF16)
        bcol = bs_ref[:, gi:gi + 1]
        for c in range(uv_ref.shape[0] // B_CHUNK):
            rs = slice(c * B_CHUNK, (c + 1) * B_CHUNK)
            cs = slice(gi * gd, (gi + 1) * gd)
            sv = jnp.dot(wg, vn[rs, cs], preferred_element_type=F32) + bcol
            o_ref[rs, cs] = (gate[rs, cs] * sv).astype(o_ref.dtype)


def _sgu(uv, zb, ln_g, ln_b, w_s, b_s):
    m, bw2 = uv.shape
    bw = bw2 // 2
    return pl.pallas_call(
        _sgu_body,
        grid=(m // ROW_TILE,),
        in_specs=[_row_tile_spec(bw2), _row_tile_spec(bw),
                  _resident_spec((1, bw)), _resident_spec((1, bw)),
                  _resident_spec(w_s.shape), _resident_spec((B_CHUNK, B_GROUPS))],
        out_specs=_row_tile_spec(bw),
        out_shape=jax.ShapeDtypeStruct((m, bw), BF16),
        compiler_params=_cparams(1),
    )(uv, zb, ln_g.reshape(1, bw), ln_b.reshape(1, bw), w_s, b_s.T)


def _merge_body(oa_ref, za_ref, ob_ref, ml_ref, wa_ref, wb_ref, o_ref):
    d = o_ref.shape[1]
    za = za_ref[...]
    oa = (oa_ref[...] * (za * _sigmoid(za))).astype(BF16)
    ua = jnp.dot(oa, wa_ref[...], preferred_element_type=F32)
    ub = jnp.dot(ob_ref[...], wb_ref[...], preferred_element_type=F32)
    o_ref[...] = (_sigmoid(ml_ref[:, :d]) * ua + _sigmoid(ml_ref[:, d:]) * ub).astype(o_ref.dtype)


def _merge(oa, za, ob, ml, w_up_a, w_up_b):
    m, aw = oa.shape
    bw = ob.shape[1]
    d = w_up_a.shape[1]
    tm = ROW_TILE // 2
    row = lambda w: pl.BlockSpec((tm, w), lambda i: (i, 0))
    return pl.pallas_call(
        _merge_body,
        grid=(m // tm,),
        in_specs=[row(aw), row(aw), row(bw), row(2 * d),
                  _resident_spec((aw, d)), _resident_spec((bw, d))],
        out_specs=row(d),
        out_shape=jax.ShapeDtypeStruct((m, d), BF16),
        compiler_params=_cparams(1),
    )(oa, za, ob, ml, w_up_a, w_up_b)


def _out_body(x_ref, mg_ref, p_ref, wo_ref, wg_ref, wp_ref, fg_ref, o_ref):
    x1 = x_ref[...] + jnp.dot(mg_ref[...], wo_ref[...], preferred_element_type=F32)
    gl = jnp.dot(x1.astype(BF16), wg_ref[...], preferred_element_type=F32)
    ple = jnp.dot(p_ref[...].astype(BF16), wp_ref[...], preferred_element_type=F32)
    x2 = x1 + _sigmoid(gl) * ple
    y = x2 * lax.rsqrt(jnp.mean(x2 * x2, axis=-1, keepdims=True) + EPS)
    o_ref[...] = y * fg_ref[...]


def _out(x2d, merged, p2d, w_out, w_gate, w_ple, final_g):
    m, d = x2d.shape
    pd = p2d.shape[1]
    tm = ROW_TILE // 2
    row = lambda w: pl.BlockSpec((tm, w), lambda i: (i, 0))
    return pl.pallas_call(
        _out_body,
        grid=(m // tm,),
        in_specs=[row(d), row(d), row(pd),
                  _resident_spec((d, d)), _resident_spec((d, d)), _resident_spec((pd, d)),
                  _resident_spec((1, d))],
        out_specs=row(d),
        out_shape=jax.ShapeDtypeStruct((m, d), F32),
        compiler_params=_cparams(1),
    )(x2d, merged, p2d, w_out, w_gate, w_ple, final_g.reshape(1, d))


def _bf16_parts(x, n):
    parts, rest = [], jnp.asarray(x, F32)
    for _ in range(n):
        piece = rest.astype(BF16)
        parts.append(piece)
        rest = rest - piece.astype(F32)
    return parts


def _attention_constants(s, ncp):
    hh = np.arange(1, A_HEADS + 1, dtype=np.float32)
    slopes = np.power(np.float32(2.0), -8.0 * hh / A_HEADS).astype(np.float32)
    slopes2 = (slopes * np.float32(LOG2E)).astype(np.float32)
    slope_col = jnp.asarray(np.repeat(slopes2.reshape(A_KV_GROUPS, A_REP), Q_BLOCK, axis=1)[:, :, None])
    slope_b = jnp.broadcast_to(slope_col, (A_KV_GROUPS, QROWS, LANES))
    qi = (jnp.arange(QROWS, dtype=jnp.int32) % Q_BLOCK).astype(F32)
    dm = jnp.arange(WIN_KEYS, dtype=F32)[None, :] - qi[:, None]
    offs = jnp.arange(WINDOW // Q_BLOCK + 1, dtype=F32)[:, None, None] * Q_BLOCK
    vis = (dm[None] <= offs) & (dm[None] > offs - WINDOW)
    bw = jnp.where(vis[None], (slope_col * dm[None])[:, None], NEG)
    tri = jnp.where(dm[:, :Q_BLOCK] <= 0, 0.0, NEG)
    c_end = jnp.arange(ncp, dtype=F32) * CMP_STRIDE + (CMP_LEN - 1)
    dmc = c_end[None, :] - qi[:, None]
    abc = slope_col * dmc[None]
    ci = np.arange(ncp)[None, :]
    sj = np.arange(LANES)[:, None]
    ovt = ((CMP_STRIDE * ci < SEL_BLOCK * sj + SEL_BLOCK) &
           (CMP_STRIDE * ci + CMP_LEN > SEL_BLOCK * sj) &
           (ci < ncp - 1)).astype(np.float32)
    parts = _bf16_parts(slopes2, N_SLOPE_PARTS)
    cols = jnp.stack([pp * POS_LO for pp in parts] + parts, axis=1)
    q_cols = jnp.pad(cols, ((0, 0), (0, HEAD_DIM - cols.shape[1]))).reshape(1, A_HEADS * HEAD_DIM)
    n_gates = 3 * A_REP
    gsp = np.zeros((LANES, n_gates * LANES), np.float32)
    for k in range(n_gates):
        gsp[k, k * LANES:k * LANES + HEAD_DIM] = 1.0
    return bw, abc, dmc, tri, slope_b, jnp.asarray(ovt, dtype=BF16), q_cols, jnp.asarray(gsp, dtype=BF16)


def kernel(x, p, norm_g, w_in, cmp_pe_k, cmp_w1_k, cmp_w2_k, cmp_pe_v, cmp_w1_v, cmp_w2_v, ln_v_g, ln_v_b, sgu_w, sgu_b, w_up_a, w_up_b, w_out, w_ple, w_ple_gate, final_g):
    b, s, d = x.shape
    assert p.shape[0] == 1, "the final norm is fused into the (single) layer's output kernel"
    m = b * s
    g, r = A_KV_GROUPS, A_REP
    ncp = s // CMP_STRIDE
    b_width = w_up_b.shape[1]
    x2d = x.reshape(m, d)
    w_t = jnp.swapaxes(w_in[0], 0, 1)

    o_kv = A_WIDTH
    o_gl = o_kv + 6 * A_KV_WIDTH
    o_za = o_gl + 3 * A_HEADS
    o_uv = o_za + A_WIDTH
    o_zb = o_uv + 2 * b_width
    o_ml = o_zb + b_width
    wcast = lambda lo, hi: w_t[lo:hi].astype(BF16)
    w_q = (w_t[:o_kv] * np.float32(HEAD_DIM ** -0.5 * LOG2E)).astype(BF16)
    w_gl = w_t[o_gl:o_za].reshape(3, g, r, d).transpose(1, 0, 2, 3).reshape(g, 3 * r, d)
    w_gl = jnp.pad(w_gl, ((0, 0), (0, LANES - 3 * r), (0, 0))).reshape(g * LANES, d).astype(BF16)

    bw, abc, dmc, tri, slope_b, ovt, q_cols, gsp = _attention_constants(s, ncp)

    h = _rmsnorm(x2d, norm_g[0])
    q_pad = _project_q(h, w_q, q_cols)
    ck, cv = _project_kvc(h, wcast(o_kv, o_kv + 2 * A_KV_WIDTH), b, s)
    ks_ext, vs_ext, kw_ext, vw_ext = _project_kvsw(h, wcast(o_kv + 2 * A_KV_WIDTH, o_gl), b, s)
    glog = _project(h, w_gl, F32)
    za = _project(h, wcast(o_za, o_uv), F32)
    uv = _project(h, wcast(o_uv, o_zb), F32)
    zb = _project(h, wcast(o_zb, o_ml), F32)
    ml = _project(h, wcast(o_ml, w_t.shape[0]), F32)

    kc, vc = _compress(ck, cv, cmp_pe_k[0], cmp_w1_k[0], cmp_w2_k[0], cmp_pe_v[0], cmp_w1_v[0], cmp_w2_v[0])
    o_att = _nsa(q_pad, glog, kc, vc, ks_ext, vs_ext, kw_ext, vw_ext, bw, abc, dmc, tri, slope_b, ovt, gsp)

    ob = _sgu(uv, zb, ln_v_g[0], ln_v_b[0], sgu_w[0], sgu_b[0])
    merged = _merge(o_att.reshape(m, A_WIDTH), za, ob, ml, w_up_a[0].astype(BF16), w_up_b[0].astype(BF16))
    out = _out(x2d, merged, p[0].reshape(m, -1), w_out[0].astype(BF16), w_ple_gate[0].astype(BF16),
               w_ple[0].astype(BF16), final_g)
    return out.reshape(b, s, d)
```

```python
import functools

import numpy as np
import jax
import jax.numpy as jnp
from jax import lax
from jax.experimental import pallas as pl
from jax.experimental.pallas import tpu as pltpu

A_HEADS = 16
A_KV_GROUPS = 4
A_REP = A_HEADS // A_KV_GROUPS
HEAD_DIM = 64
A_WIDTH = A_HEADS * HEAD_DIM
A_KV_WIDTH = A_KV_GROUPS * HEAD_DIM
CMP_LEN = 32
CMP_STRIDE = 16
SEL_BLOCK = 64
SEL_TOP_N = 16
WINDOW = 512
Q_BLOCK = 128
FORCE_BONUS = 1000.0
B_GROUPS = 8
B_CHUNK = 128
NEG = -1e30
EPS = 1e-6
LOG2E = float(np.log2(np.e))

LANES = 128
QROWS = A_REP * Q_BLOCK
NSA_BLOCKS = 2
SEL_TILE = 1024
SEL_CHUNKS = SEL_TILE // LANES
WIN_KEYS = WINDOW + Q_BLOCK
ROW_TILE = 512
PROJ_TILE = 1024
VMEM_LIMIT = 56 * 1024 * 1024
POS_SHIFT = 5
POS_LO = 1 << POS_SHIFT
N_SLOPE_PARTS = 3

F32 = jnp.float32
BF16 = jnp.bfloat16


def _cparams(n_axes):
    return pltpu.CompilerParams(dimension_semantics=("arbitrary",) * n_axes,
                                vmem_limit_bytes=VMEM_LIMIT)


def _sigmoid(x):
    return 1.0 / (1.0 + jnp.exp(-x))


def _dot_nt(a, b):
    return lax.dot_general(a, b, (((1,), (1,)), ((), ())), preferred_element_type=F32)


def _row_tile_spec(width):
    return pl.BlockSpec((ROW_TILE, width), lambda i: (i, 0))


def _resident_spec(shape):
    return pl.BlockSpec(shape, lambda i: (0,) * len(shape), pipeline_mode=pl.Buffered(1))


def _norm_body(x_ref, g_ref, h_ref):
    x = x_ref[...]
    y = x * lax.rsqrt(jnp.mean(x * x, axis=-1, keepdims=True) + EPS)
    h_ref[...] = (y * g_ref[...]).astype(h_ref.dtype)


def _rmsnorm(x2d, g):
    m, d = x2d.shape
    return pl.pallas_call(
        _norm_body,
        grid=(m // ROW_TILE,),
        in_specs=[_row_tile_spec(d), _resident_spec((1, d))],
        out_specs=_row_tile_spec(d),
        out_shape=jax.ShapeDtypeStruct((m, d), BF16),
        compiler_params=_cparams(1),
    )(x2d, g.reshape(1, d))


def _mm_body(h_ref, w_ref, o_ref):
    o_ref[...] = _dot_nt(h_ref[...], w_ref[...]).astype(o_ref.dtype)


def _project(h, w, out_dtype):
    m, k = h.shape
    n = w.shape[0]
    tn = min(n, 2048)
    return pl.pallas_call(
        _mm_body,
        grid=(n // tn, m // PROJ_TILE),
        in_specs=[pl.BlockSpec((PROJ_TILE, k), lambda j, i: (i, 0)),
                  pl.BlockSpec((tn, k), lambda j, i: (j, 0))],
        out_specs=pl.BlockSpec((PROJ_TILE, tn), lambda j, i: (i, j)),
        out_shape=jax.ShapeDtypeStruct((m, n), out_dtype),
        compiler_params=_cparams(2),
    )(h, w)


def _q_body(h_ref, w_ref, qc_ref, o_ref):
    res = _dot_nt(h_ref[...], w_ref[...])
    rows = res.shape[0]
    for hd in range(A_HEADS):
        cs = slice(hd * HEAD_DIM, (hd + 1) * HEAD_DIM)
        cst = jnp.broadcast_to(qc_ref[:, cs], (rows, HEAD_DIM))
        o_ref[:, hd * LANES:(hd + 1) * LANES] = jnp.concatenate([res[:, cs].astype(BF16), cst], axis=1)


def _project_q(h, w_q, q_cols):
    m, k = h.shape
    return pl.pallas_call(
        _q_body,
        grid=(m // ROW_TILE,),
        in_specs=[_row_tile_spec(k), _resident_spec(w_q.shape), _resident_spec(q_cols.shape)],
        out_specs=_row_tile_spec(A_HEADS * LANES),
        out_shape=jax.ShapeDtypeStruct((m, A_HEADS * LANES), BF16),
        compiler_params=_cparams(1),
    )(h, w_q, q_cols)


def _kvsw_body(seq, h_ref, w_ref, ks_ref, vs_ref, kw_ref, vw_ref):
    res = _dot_nt(h_ref[...], w_ref[...])
    rows = res.shape[0]
    tok0 = (pl.program_id(0) * rows) % seq
    pos = tok0 + lax.broadcasted_iota(jnp.int32, (rows, LANES), 0)
    lane = lax.broadcasted_iota(jnp.int32, (rows, LANES), 1)
    onehot = jnp.where((pos >> 6) == lane, 1.0, 0.0).astype(BF16)
    off = (tok0 + lax.broadcasted_iota(jnp.int32, (rows, HEAD_DIM), 0)) & (SEL_TILE - 1)
    lane_h = lax.broadcasted_iota(jnp.int32, (rows, HEAD_DIM), 1)
    pos_cols = jnp.where(lane_h < N_SLOPE_PARTS, off >> POS_SHIFT,
                         jnp.where(lane_h < 2 * N_SLOPE_PARTS, off & (POS_LO - 1), 0)).astype(F32).astype(BF16)
    ones_col = jnp.where(lane_h == 0, 1.0, 0.0).astype(BF16)
    zeros = jnp.zeros((rows, HEAD_DIM), BF16)
    piece = lambda j: res[:, j * HEAD_DIM:(j + 1) * HEAD_DIM].astype(BF16)
    for g in range(A_KV_GROUPS):
        ks_ref[g, :, 0:LANES] = onehot
        ks_ref[g, :, LANES:2 * LANES] = jnp.concatenate([piece(g), pos_cols], axis=1)
        vs_ref[g] = jnp.concatenate([piece(A_KV_GROUPS + g), ones_col], axis=1)
        kw_ref[g] = jnp.concatenate([piece(2 * A_KV_GROUPS + g), zeros], axis=1)
        vw_ref[g] = jnp.concatenate([piece(3 * A_KV_GROUPS + g), ones_col], axis=1)


def _project_kvsw(h, w, b, s):
    m, k = h.shape
    tps = s // ROW_TILE
    assert s // SEL_BLOCK <= LANES, "one-hot selection columns must fit one lane tile"
    spec = lambda width: pl.BlockSpec((None, A_KV_GROUPS, ROW_TILE, width),
                                      lambda i: (i // tps, 0, i % tps, 0))
    shape = lambda width: jax.ShapeDtypeStruct((b, A_KV_GROUPS, s, width), BF16)
    return pl.pallas_call(
        lambda *refs: _kvsw_body(s, *refs),
        grid=(m // ROW_TILE,),
        in_specs=[_row_tile_spec(k), _resident_spec(w.shape)],
        out_specs=[spec(2 * LANES), spec(LANES), spec(LANES), spec(LANES)],
        out_shape=[shape(2 * LANES), shape(LANES), shape(LANES), shape(LANES)],
        compiler_params=_cparams(1),
    )(h, w)


def _kvc_body(h_ref, w_ref, ck_ref, cv_ref, res_sc):
    res = _dot_nt(h_ref[...], w_ref[...])
    n_slabs = res_sc.shape[0]
    for c in range(n_slabs):
        res_sc[c] = res[:, c * LANES:(c + 1) * LANES]
    nrow = res_sc.shape[1] // CMP_STRIDE
    heads_per_slab = LANES // HEAD_DIM
    for c in range(n_slabs):
        out_ref = ck_ref if c < n_slabs // 2 else cv_ref
        g0 = (c % (n_slabs // 2)) * heads_per_slab
        for l in range(CMP_STRIDE):
            slab = res_sc[c, pl.ds(l, nrow, stride=CMP_STRIDE), :]
            for j in range(heads_per_slab):
                out_ref[g0 + j, :, l * HEAD_DIM:(l + 1) * HEAD_DIM] = slab[:, j * HEAD_DIM:(j + 1) * HEAD_DIM]


def _project_kvc(h, w, b, s):
    m, k = h.shape
    tps = s // ROW_TILE
    nrow = ROW_TILE // CMP_STRIDE
    width = CMP_STRIDE * HEAD_DIM
    spec = pl.BlockSpec((None, A_KV_GROUPS, nrow, width), lambda i: (i // tps, 0, i % tps, 0))
    shape = jax.ShapeDtypeStruct((b, A_KV_GROUPS, s // CMP_STRIDE, width), F32)
    return pl.pallas_call(
        _kvc_body,
        grid=(m // ROW_TILE,),
        in_specs=[_row_tile_spec(k), _resident_spec(w.shape)],
        out_specs=[spec, spec],
        out_shape=[shape, shape],
        scratch_shapes=[pltpu.VMEM((w.shape[0] // LANES, ROW_TILE, LANES), F32)],
        compiler_params=_cparams(1),
    )(h, w)


def _compress_one(c, pe, w1, w2p, ncp):
    half = CMP_STRIDE * HEAD_DIM
    hp = lax.Precision.HIGHEST
    h_lo = jnp.dot(c + pe[:, :half], w1[:half, :], preferred_element_type=F32, precision=hp)
    h_hi = jnp.dot(c + pe[:, half:], w1[half:, :], preferred_element_type=F32, precision=hp)
    pre = h_lo + pltpu.roll(h_hi, ncp - 1, 0)
    hid = pre * _sigmoid(pre)
    out = jnp.dot(hid, w2p, preferred_element_type=F32, precision=hp)
    row = lax.broadcasted_iota(jnp.int32, out.shape, 0)
    return jnp.where(row < ncp - 1, out, 0.0)


def _compress_body(ck_ref, cv_ref, pek_ref, w1k_ref, w2k_ref, pev_ref, w1v_ref, w2v_ref,
                   kc_ref, vc_ref):
    ncp = ck_ref.shape[0]
    kc = _compress_one(ck_ref[...], pek_ref[...], w1k_ref[...], w2k_ref[...], ncp)
    vc = _compress_one(cv_ref[...], pev_ref[...], w1v_ref[...], w2v_ref[...], ncp)
    kc_ref[...] = kc.astype(kc_ref.dtype)
    vc_ref[...] = vc.astype(vc_ref.dtype)


def _compress(ck, cv, pe_k, w1_k, w2_k, pe_v, w1_v, w2_v):
    b, g, ncp, cw = ck.shape
    flat = CMP_LEN * HEAD_DIM
    pad2 = lambda w: jnp.pad(w, ((0, 0), (0, LANES - HEAD_DIM)))
    tok = pl.BlockSpec((None, None, ncp, cw), lambda bi, gi: (bi, gi, 0, 0))
    full = lambda shp: pl.BlockSpec(shp, lambda bi, gi: (0,) * len(shp))
    out = pl.BlockSpec((None, None, ncp, LANES), lambda bi, gi: (bi, gi, 0, 0))
    return pl.pallas_call(
        _compress_body,
        grid=(b, g),
        in_specs=[tok, tok,
                  full((1, flat)), full((flat, HEAD_DIM)), full((HEAD_DIM, LANES)),
                  full((1, flat)), full((flat, HEAD_DIM)), full((HEAD_DIM, LANES))],
        out_specs=[out, out],
        out_shape=[jax.ShapeDtypeStruct((b, g, ncp, LANES), BF16)] * 2,
        compiler_params=_cparams(2),
    )(ck, cv,
      pe_k.reshape(1, flat), w1_k.reshape(flat, HEAD_DIM), pad2(w2_k),
      pe_v.reshape(1, flat), w1_v.reshape(flat, HEAD_DIM), pad2(w2_v))


def _nsa_front(n_top, first, qs, kc_ref, vc_ref, ks_ref, vs_ref, kw_ref, vw_ref,
               bw_refs, abc_ref, dmc_ref, tri_ref, slope, ovt_ref, m_sc, acc_sc, qa_sc, used_ref):
    nb = len(qs)
    q0s = [(first + k) * Q_BLOCK for k in range(nb)]
    n_sel = ovt_ref.shape[0]
    rows_of = lambda k: slice(k * QROWS, (k + 1) * QROWS)
    stack = lambda parts: jnp.concatenate(parts, axis=0)

    s_all = _dot_nt(stack(qs), kc_ref[...])
    dmc = dmc_ref[...]
    abc = abc_ref[...]
    sc = stack([jnp.where(dmc <= q0s[k].astype(F32), s_all[rows_of(k)] + abc, NEG) for k in range(nb)])
    mc = jnp.max(sc, axis=1, keepdims=True)
    pc = jnp.exp2(sc - mc)
    lc = jnp.sum(pc, axis=1, keepdims=True)
    pc = pc * jnp.where(mc > 0.5 * NEG, 1.0 / lc, 0.0)
    o_c = jnp.dot(pc.astype(BF16), vc_ref[...], preferred_element_type=F32)[:, :HEAD_DIM]

    head_sum = lambda k: functools.reduce(
        jnp.add, [pc[k * QROWS + r * Q_BLOCK:k * QROWS + (r + 1) * Q_BLOCK] for r in range(A_REP)])
    psum = stack([head_sum(k) for k in range(nb)])
    p1 = psum.astype(BF16)
    r1 = psum - p1.astype(F32)
    p2 = r1.astype(BF16)
    p3 = (r1 - p2.astype(F32)).astype(BF16)
    ovt = ovt_ref[...]
    imp_t = _dot_nt(ovt, p3) + _dot_nt(ovt, p2) + _dot_nt(ovt, p1)

    kb_w = [pl.multiple_of(jnp.maximum(q0 - WINDOW, 0), Q_BLOCK) for q0 in q0s]
    sw = stack([_dot_nt(qs[k], kw_ref[pl.ds(kb_w[k], WIN_KEYS), :]) + bw_refs[k][...] for k in range(nb)])
    pw = jnp.exp2(sw - jnp.max(sw, axis=1, keepdims=True)).astype(BF16)
    accw = stack([jnp.dot(pw[rows_of(k)], vw_ref[pl.ds(kb_w[k], WIN_KEYS), :], preferred_element_type=F32)
                  for k in range(nb)])
    o_w = accw[:, :HEAD_DIM] / accw[:, HEAD_DIM:HEAD_DIM + 1]

    q0a = [pl.multiple_of(q0, Q_BLOCK) for q0 in q0s]
    tri = tri_ref[...]
    s_d = stack([_dot_nt(qs[k], ks_ref[pl.ds(q0a[k], Q_BLOCK), LANES:2 * LANES]) + tri for k in range(nb)])
    mx_d = jnp.broadcast_to(jnp.max(s_d, axis=1, keepdims=True), (nb * QROWS, LANES))
    p_d = jnp.exp2(s_d - mx_d).astype(BF16)
    for k in range(nb):
        acc_sc[k] = jnp.dot(p_d[rows_of(k)], vs_ref[pl.ds(q0a[k], Q_BLOCK), :], preferred_element_type=F32)
        kb_d = (q0s[k] // SEL_TILE) * SEL_TILE
        m_sc[k] = mx_d[rows_of(k)] + slope * (kb_d - q0s[k]).astype(F32)

    blk = lax.broadcasted_iota(jnp.int32, (n_sel, nb * Q_BLOCK), 0)
    lane_q = lax.broadcasted_iota(jnp.int32, (n_sel, nb * Q_BLOCK), 1)
    t_l = q0s[0] + lane_q
    cur = t_l >> 6
    causal_b = blk * SEL_BLOCK <= t_l
    forced = (blk == 0) | (blk == cur) | (blk == cur - 1)
    score = jnp.where(causal_b & jnp.logical_not(forced), imp_t, -1.0)
    lanes_of = lambda k: slice(k * Q_BLOCK, (k + 1) * Q_BLOCK)
    blk_f = lax.broadcasted_iota(jnp.int32, (n_sel, Q_BLOCK), 0).astype(F32)
    scores = [score[:, lanes_of(k)] for k in range(nb)]
    for _ in range(n_top - 3):
        for k in range(nb):
            top = jnp.max(scores[k], axis=0, keepdims=True)
            first_blk = jnp.min(jnp.where(scores[k] == top, blk_f, float(n_sel)), axis=0, keepdims=True)
            scores[k] = jnp.where(blk_f == first_blk, -3e38, scores[k])
    sel = jnp.where(forced | (jnp.concatenate(scores, axis=1) < -1e38), 1.0, 0.0)
    q0_l = t_l & ~(Q_BLOCK - 1)
    in_loop = (sel > 0.0) & (blk * SEL_BLOCK < q0_l)
    bias_t = jnp.where(in_loop, 0.0, NEG)
    used = jnp.where(in_loop, 1.0, 0.0)
    blocks_per_tile = SEL_TILE // SEL_BLOCK
    for t in range(used_ref.shape[0]):
        used_ref[t] = jnp.max(used[t * blocks_per_tile:(t + 1) * blocks_per_tile]).astype(jnp.int32)
    bias = bias_t.T.astype(BF16)
    for k in range(nb):
        bias_k = bias[k * Q_BLOCK:(k + 1) * Q_BLOCK]
        qa_sc[k] = jnp.concatenate([jnp.concatenate([bias_k] * A_REP, axis=0), qs[k]], axis=1)
    return o_c, o_w


def _nsa_body(n_top, q_ref, gl_ref, kc_ref, vc_ref, ks_ref, vs_ref, kw_ref, vw_ref, *rest):
    bw_refs = rest[:NSA_BLOCKS]
    (abc_ref, dmc_ref, tri_ref, slope_ref, ovt_ref, gsp_ref, o_ref,
     m_sc, acc_sc, qa_sc, used_ref) = rest[NSA_BLOCKS:]
    first = pl.program_id(2) * NSA_BLOCKS
    slope = slope_ref[...]
    qb = q_ref[...]
    qs = [jnp.concatenate([qb[k * Q_BLOCK:(k + 1) * Q_BLOCK, r * LANES:(r + 1) * LANES]
                           for r in range(A_REP)], axis=0) for k in range(NSA_BLOCKS)]
    o_c, o_w = _nsa_front(n_top, first, qs, kc_ref, vc_ref, ks_ref, vs_ref, kw_ref, vw_ref,
                          bw_refs, abc_ref, dmc_ref, tri_ref, slope, ovt_ref, m_sc, acc_sc, qa_sc, used_ref)

    def sel_tile(tile, carry):
        @pl.when(used_ref[tile] > 0)
        def _():
            kb = pl.multiple_of(tile * SEL_TILE, SEL_TILE)
            scores = [_dot_nt(qa_sc[k], ks_ref[pl.ds(kb, SEL_TILE), :]) for k in range(NSA_BLOCKS)]
            for k in range(NSA_BLOCKS):
                s = scores[k]
                c = slope * (kb - (first + k) * Q_BLOCK).astype(F32)
                mx = s[:, 0:LANES]
                for j in range(1, SEL_CHUNKS):
                    mx = jnp.maximum(mx, s[:, j * LANES:(j + 1) * LANES])
                m_old = m_sc[k]
                m_new = jnp.maximum(m_old, jnp.max(mx, axis=1, keepdims=True) + c)
                shift = m_new - c
                p = jnp.concatenate([jnp.exp2(s[:, j * LANES:(j + 1) * LANES] - shift).astype(BF16)
                                     for j in range(SEL_CHUNKS)], axis=1)
                pv = jnp.dot(p, vs_ref[pl.ds(kb, SEL_TILE), :], preferred_element_type=F32)
                acc_sc[k] = jnp.exp2(m_old - m_new) * acc_sc[k] + pv
                m_sc[k] = m_new
        return carry

    q0_last = (first + NSA_BLOCKS - 1) * Q_BLOCK
    lax.fori_loop(0, (q0_last + SEL_TILE - 1) // SEL_TILE, sel_tile, 0)

    gate = _sigmoid(gl_ref[...])
    g_hi = gate.astype(BF16)
    g_lo = (gate - g_hi.astype(F32)).astype(BF16)
    g_rep = jnp.dot(jnp.concatenate([g_hi, g_lo], axis=1), gsp_ref[...],
                    preferred_element_type=F32)
    for k in range(NSA_BLOCKS):
        acc = acc_sc[k]
        o_s = acc[:, :HEAD_DIM] / acc[:, HEAD_DIM:HEAD_DIM + 1]
        q_rows = slice(k * Q_BLOCK, (k + 1) * Q_BLOCK)
        g_slot = lambda n: g_rep[q_rows, n * LANES:n * LANES + HEAD_DIM]
        for r in range(A_REP):
            rs = slice(r * Q_BLOCK, (r + 1) * Q_BLOCK)
            rk = slice(k * QROWS + r * Q_BLOCK, k * QROWS + (r + 1) * Q_BLOCK)
            o_ref[q_rows, r * HEAD_DIM:(r + 1) * HEAD_DIM] = (
                g_slot(r) * o_c[rk] + g_slot(A_REP + r) * o_s[rs] + g_slot(2 * A_REP + r) * o_w[rk])


def _nsa(q_pad, glog, kc, vc, ks_ext, vs_ext, kw_ext, vw_ext, bw, abc, dmc, tri, slope_b, ovt, gsp):
    b, g, s = ks_ext.shape[:3]
    nq = s // Q_BLOCK
    steps = nq // NSA_BLOCKS
    rows = NSA_BLOCKS * Q_BLOCK
    ncp = kc.shape[2]
    n_top = min(SEL_TOP_N, s // SEL_BLOCK)
    assert n_top >= 3
    n_win_cases = bw.shape[1]
    per_bg = lambda shp: pl.BlockSpec((None, None) + shp, lambda bi, gi, i: (bi, gi, 0, 0))
    per_g = lambda shp: pl.BlockSpec((None,) + shp, lambda bi, gi, i: (gi, 0, 0))
    const = lambda shp: pl.BlockSpec(shp, lambda bi, gi, i: (0, 0))
    win_case = lambda k: pl.BlockSpec(
        (None, None, QROWS, WIN_KEYS),
        lambda bi, gi, i: (gi, jnp.minimum(i * NSA_BLOCKS + k, n_win_cases - 1), 0, 0))
    return pl.pallas_call(
        functools.partial(_nsa_body, n_top),
        grid=(b, g, steps),
        in_specs=[pl.BlockSpec((rows, A_REP * LANES), lambda bi, gi, i: (bi * steps + i, gi)),
                  pl.BlockSpec((rows, LANES), lambda bi, gi, i: (bi * steps + i, gi)),
                  per_bg((ncp, LANES)), per_bg((ncp, LANES)),
                  per_bg((s, ks_ext.shape[3])), per_bg((s, LANES)),
                  per_bg((s, LANES)), per_bg((s, LANES))]
                 + [win_case(k) for k in range(NSA_BLOCKS)]
                 + [per_g((QROWS, ncp)), const((QROWS, ncp)), const((QROWS, Q_BLOCK)),
                    per_g((QROWS, LANES)), const(ovt.shape), const(gsp.shape)],
        out_specs=pl.BlockSpec((None, rows, A_REP * HEAD_DIM), lambda bi, gi, i: (bi, i, gi)),
        out_shape=jax.ShapeDtypeStruct((b, s, A_WIDTH), F32),
        scratch_shapes=[pltpu.VMEM((NSA_BLOCKS, QROWS, LANES), F32),
                        pltpu.VMEM((NSA_BLOCKS, QROWS, LANES), F32),
                        pltpu.VMEM((NSA_BLOCKS, QROWS, 2 * LANES), BF16),
                        pltpu.SMEM((s // SEL_TILE,), jnp.int32)],
        compiler_params=_cparams(3),
    )(q_pad, glog, kc, vc, ks_ext, vs_ext, kw_ext, vw_ext, *([bw] * NSA_BLOCKS),
      abc, dmc, tri, slope_b, ovt, gsp)


def _gelu(x):
    return 0.5 * x * (1.0 + jnp.tanh(np.float32(np.sqrt(2.0 / np.pi)) * (x + 0.044715 * (x * x * x))))


def _sgu_body(uv_ref, zb_ref, lng_ref, lnb_ref, w_ref, bs_ref, o_ref):
    bw = zb_ref.shape[1]
    gd = bw // B_GROUPS
    u = _gelu(uv_ref[:, :bw])
    v = _gelu(uv_ref[:, bw:])
    mu = jnp.mean(v, axis=-1, keepdims=True)
    vc = v - mu
    vn = vc * lax.rsqrt(jnp.mean(vc * vc, axis=-1, keepdims=True) + EPS)
    vn = (vn * lng_ref[...] + lnb_ref[...]).astype(BF16)
    zb = zb_ref[...]
    gate = u * (zb * _sigmoid(zb))
    ti = lax.broadcasted_iota(jnp.int32, (B_CHUNK, B_CHUNK), 0)
    si = lax.broadcasted_iota(jnp.int32, (B_CHUNK, B_CHUNK), 1)
    for gi in range(B_GROUPS):
        wg = jnp.where(si <= ti, w_ref[gi], 0.0).astype(BF16)
        bcol = bs_ref[:, gi:gi + 1]
        for c in range(uv_ref.shape[0] // B_CHUNK):
            rs = slice(c * B_CHUNK, (c + 1) * B_CHUNK)
            cs = slice(gi * gd, (gi + 1) * gd)
            sv = jnp.dot(wg, vn[rs, cs], preferred_element_type=F32) + bcol
            o_ref[rs, cs] = (gate[rs, cs] * sv).astype(o_ref.dtype)


def _sgu(uv, zb, ln_g, ln_b, w_s, b_s):
    m, bw2 = uv.shape
    bw = bw2 // 2
    return pl.pallas_call(
        _sgu_body,
        grid=(m // ROW_TILE,),
        in_specs=[_row_tile_spec(bw2), _row_tile_spec(bw),
                  _resident_spec((1, bw)), _resident_spec((1, bw)),
                  _resident_spec(w_s.shape), _resident_spec((B_CHUNK, B_GROUPS))],
        out_specs=_row_tile_spec(bw),
        out_shape=jax.ShapeDtypeStruct((m, bw), BF16),
        compiler_params=_cparams(1),
    )(uv, zb, ln_g.reshape(1, bw), ln_b.reshape(1, bw), w_s, b_s.T)


def _merge_body(oa_ref, za_ref, ob_ref, ml_ref, wa_ref, wb_ref, o_ref):
    d = o_ref.shape[1]
    za = za_ref[...]
    oa = (oa_ref[...] * (za * _sigmoid(za))).astype(BF16)
    ua = jnp.dot(oa, wa_ref[...], preferred_element_type=F32)
    ub = jnp.dot(ob_ref[...], wb_ref[...], preferred_element_type=F32)
    o_ref[...] = (_sigmoid(ml_ref[:, :d]) * ua + _sigmoid(ml_ref[:, d:]) * ub).astype(o_ref.dtype)


def _merge(oa, za, ob, ml, w_up_a, w_up_b):
    m, aw = oa.shape
    bw = ob.shape[1]
    d = w_up_a.shape[1]
    tm = ROW_TILE
    row = lambda w: pl.BlockSpec((tm, w), lambda i: (i, 0))
    return pl.pallas_call(
        _merge_body,
        grid=(m // tm,),
        in_specs=[row(aw), row(aw), row(bw), row(2 * d),
                  _resident_spec((aw, d)), _resident_spec((bw, d))],
        out_specs=row(d),
        out_shape=jax.ShapeDtypeStruct((m, d), BF16),
        compiler_params=_cparams(1),
    )(oa, za, ob, ml, w_up_a, w_up_b)


def _out_body(x_ref, mg_ref, p_ref, wo_ref, wg_ref, wp_ref, fg_ref, o_ref):
    x1 = x_ref[...] + jnp.dot(mg_ref[...], wo_ref[...], preferred_element_type=F32)
    gl = jnp.dot(x1.astype(BF16), wg_ref[...], preferred_element_type=F32)
    ple = jnp.dot(p_ref[...].astype(BF16), wp_ref[...], preferred_element_type=F32)
    x2 = x1 + _sigmoid(gl) * ple
    y = x2 * lax.rsqrt(jnp.mean(x2 * x2, axis=-1, keepdims=True) + EPS)
    o_ref[...] = y * fg_ref[...]


def _out(x2d, merged, p2d, w_out, w_gate, w_ple, final_g):
    m, d = x2d.shape
    pd = p2d.shape[1]
    tm = ROW_TILE
    row = lambda w: pl.BlockSpec((tm, w), lambda i: (i, 0))
    return pl.pallas_call(
        _out_body,
        grid=(m // tm,),
        in_specs=[row(d), row(d), row(pd),
                  _resident_spec((d, d)), _resident_spec((d, d)), _resident_spec((pd, d)),
                  _resident_spec((1, d))],
        out_specs=row(d),
        out_shape=jax.ShapeDtypeStruct((m, d), F32),
        compiler_params=_cparams(1),
    )(x2d, merged, p2d, w_out, w_gate, w_ple, final_g.reshape(1, d))


def _bf16_parts(x, n):
    parts, rest = [], jnp.asarray(x, F32)
    for _ in range(n):
        piece = rest.astype(BF16)
        parts.append(piece)
        rest = rest - piece.astype(F32)
    return parts


def _attention_constants(s, ncp):
    hh = np.arange(1, A_HEADS + 1, dtype=np.float32)
    slopes = np.power(np.float32(2.0), -8.0 * hh / A_HEADS).astype(np.float32)
    slopes2 = (slopes * np.float32(LOG2E)).astype(np.float32)
    slope_col = jnp.asarray(np.repeat(slopes2.reshape(A_KV_GROUPS, A_REP), Q_BLOCK, axis=1)[:, :, None])
    slope_b = jnp.broadcast_to(slope_col, (A_KV_GROUPS, QROWS, LANES))
    qi = (jnp.arange(QROWS, dtype=jnp.int32) % Q_BLOCK).astype(F32)
    dm = jnp.arange(WIN_KEYS, dtype=F32)[None, :] - qi[:, None]
    offs = jnp.arange(WINDOW // Q_BLOCK + 1, dtype=F32)[:, None, None] * Q_BLOCK
    vis = (dm[None] <= offs) & (dm[None] > offs - WINDOW)
    bw = jnp.where(vis[None], (slope_col * dm[None])[:, None], NEG)
    tri = jnp.where(dm[:, :Q_BLOCK] <= 0, 0.0, NEG)
    c_end = jnp.arange(ncp, dtype=F32) * CMP_STRIDE + (CMP_LEN - 1)
    dmc = c_end[None, :] - qi[:, None]
    abc = slope_col * dmc[None]
    ci = np.arange(ncp)[None, :]
    sj = np.arange(LANES)[:, None]
    ovt = ((CMP_STRIDE * ci < SEL_BLOCK * sj + SEL_BLOCK) &
           (CMP_STRIDE * ci + CMP_LEN > SEL_BLOCK * sj) &
           (ci < ncp - 1)).astype(np.float32)
    parts = _bf16_parts(slopes2, N_SLOPE_PARTS)
    cols = jnp.stack([pp * POS_LO for pp in parts] + parts, axis=1)
    q_cols = jnp.pad(cols, ((0, 0), (0, HEAD_DIM - cols.shape[1]))).reshape(1, A_HEADS * HEAD_DIM)
    n_gates = 3 * A_REP
    gsp = np.zeros((2 * LANES, n_gates * LANES), np.float32)
    for k in range(n_gates):
        gsp[k, k * LANES:k * LANES + HEAD_DIM] = 1.0
        gsp[LANES + k, k * LANES:k * LANES + HEAD_DIM] = 1.0
    return bw, abc, dmc, tri, slope_b, jnp.asarray(ovt, dtype=BF16), q_cols, jnp.asarray(gsp, dtype=BF16)


def kernel(x, p, norm_g, w_in, cmp_pe_k, cmp_w1_k, cmp_w2_k, cmp_pe_v, cmp_w1_v, cmp_w2_v, ln_v_g, ln_v_b, sgu_w, sgu_b, w_up_a, w_up_b, w_out, w_ple, w_ple_gate, final_g):
    b, s, d = x.shape
    assert p.shape[0] == 1, "the final norm is fused into the (single) layer's output kernel"
    m = b * s
    g, r = A_KV_GROUPS, A_REP
    ncp = s // CMP_STRIDE
    b_width = w_up_b.shape[1]
    x2d = x.reshape(m, d)
    w_t = jnp.swapaxes(w_in[0], 0, 1)

    o_kv = A_WIDTH
    o_gl = o_kv + 6 * A_KV_WIDTH
    o_za = o_gl + 3 * A_HEADS
    o_uv = o_za + A_WIDTH
    o_zb = o_uv + 2 * b_width
    o_ml = o_zb + b_width
    wcast = lambda lo, hi: w_t[lo:hi].astype(BF16)
    w_q = (w_t[:o_kv] * np.float32(HEAD_DIM ** -0.5 * LOG2E)).astype(BF16)
    w_gl = w_t[o_gl:o_za].reshape(3, g, r, d).transpose(1, 0, 2, 3).reshape(g, 3 * r, d)
    w_gl = jnp.pad(w_gl, ((0, 0), (0, LANES - 3 * r), (0, 0))).reshape(g * LANES, d).astype(BF16)

    bw, abc, dmc, tri, slope_b, ovt, q_cols, gsp = _attention_constants(s, ncp)

    h = _rmsnorm(x2d, norm_g[0])
    q_pad = _project_q(h, w_q, q_cols)
    ck, cv = _project_kvc(h, wcast(o_kv, o_kv + 2 * A_KV_WIDTH), b, s)
    ks_ext, vs_ext, kw_ext, vw_ext = _project_kvsw(h, wcast(o_kv + 2 * A_KV_WIDTH, o_gl), b, s)
    glog = _project(h, w_gl, F32)
    za = _project(h, wcast(o_za, o_uv), F32)
    uv = _project(h, wcast(o_uv, o_zb), F32)
    zb = _project(h, wcast(o_zb, o_ml), F32)
    ml = _project(h, wcast(o_ml, w_t.shape[0]), F32)

    kc, vc = _compress(ck, cv, cmp_pe_k[0], cmp_w1_k[0], cmp_w2_k[0], cmp_pe_v[0], cmp_w1_v[0], cmp_w2_v[0])
    o_att = _nsa(q_pad, glog, kc, vc, ks_ext, vs_ext, kw_ext, vw_ext, bw, abc, dmc, tri, slope_b, ovt, gsp)

    ob = _sgu(uv, zb, ln_v_g[0], ln_v_b[0], sgu_w[0], sgu_b[0])
    merged = _merge(o_att.reshape(m, A_WIDTH), za, ob, ml, w_up_a[0].astype(BF16), w_up_b[0].astype(BF16))
    out = _out(x2d, merged, p[0].reshape(m, -1), w_out[0].astype(BF16), w_ple_gate[0].astype(BF16),
               w_ple[0].astype(BF16), final_g)
    return out.reshape(b, s, d)
```

```python
import functools

import numpy as np
import jax
import jax.numpy as jnp
from jax import lax
from jax.experimental import pallas as pl
from jax.experimental.pallas import tpu as pltpu

A_HEADS = 16
A_KV_GROUPS = 4
A_REP = A_HEADS // A_KV_GROUPS
HEAD_DIM = 64
A_WIDTH = A_HEADS * HEAD_DIM
A_KV_WIDTH = A_KV_GROUPS * HEAD_DIM
CMP_LEN = 32
CMP_STRIDE = 16
SEL_BLOCK = 64
SEL_TOP_N = 16
WINDOW = 512
Q_BLOCK = 128
FORCE_BONUS = 1000.0
B_GROUPS = 8
B_CHUNK = 128
NEG = -1e30
EPS = 1e-6
LOG2E = float(np.log2(np.e))

LANES = 128
QROWS = A_REP * Q_BLOCK
NSA_BLOCKS = 2
SEL_TILE = 1024
SEL_CHUNKS = SEL_TILE // LANES
WIN_KEYS = WINDOW + Q_BLOCK
ROW_TILE = 512
PROJ_TILE = 1024
VMEM_LIMIT = 56 * 1024 * 1024
POS_SHIFT = 5
POS_LO = 1 << POS_SHIFT
N_SLOPE_PARTS = 3

F32 = jnp.float32
BF16 = jnp.bfloat16


def _cparams(n_axes):
    return pltpu.CompilerParams(dimension_semantics=("arbitrary",) * n_axes,
                                vmem_limit_bytes=VMEM_LIMIT)


def _sigmoid(x):
    return 1.0 / (1.0 + jnp.exp(-x))


def _dot_nt(a, b):
    return lax.dot_general(a, b, (((1,), (1,)), ((), ())), preferred_element_type=F32)


def _row_tile_spec(width):
    return pl.BlockSpec((ROW_TILE, width), lambda i: (i, 0))


def _resident_spec(shape):
    return pl.BlockSpec(shape, lambda i: (0,) * len(shape), pipeline_mode=pl.Buffered(1))


def _norm_body(x_ref, g_ref, h_ref):
    x = x_ref[...]
    y = x * lax.rsqrt(jnp.mean(x * x, axis=-1, keepdims=True) + EPS)
    h_ref[...] = (y * g_ref[...]).astype(h_ref.dtype)


def _rmsnorm(x2d, g):
    m, d = x2d.shape
    return pl.pallas_call(
        _norm_body,
        grid=(m // ROW_TILE,),
        in_specs=[_row_tile_spec(d), _resident_spec((1, d))],
        out_specs=_row_tile_spec(d),
        out_shape=jax.ShapeDtypeStruct((m, d), BF16),
        compiler_params=_cparams(1),
    )(x2d, g.reshape(1, d))


def _mm_body(h_ref, w_ref, o_ref):
    o_ref[...] = _dot_nt(h_ref[...], w_ref[...]).astype(o_ref.dtype)


def _project(h, w, out_dtype):
    m, k = h.shape
    n = w.shape[0]
    tn = min(n, 2048)
    return pl.pallas_call(
        _mm_body,
        grid=(n // tn, m // PROJ_TILE),
        in_specs=[pl.BlockSpec((PROJ_TILE, k), lambda j, i: (i, 0)),
                  pl.BlockSpec((tn, k), lambda j, i: (j, 0))],
        out_specs=pl.BlockSpec((PROJ_TILE, tn), lambda j, i: (i, j)),
        out_shape=jax.ShapeDtypeStruct((m, n), out_dtype),
        compiler_params=_cparams(2),
    )(h, w)


def _q_body(h_ref, w_ref, qc_ref, o_ref):
    res = _dot_nt(h_ref[...], w_ref[...])
    rows = res.shape[0]
    for hd in range(A_HEADS):
        cs = slice(hd * HEAD_DIM, (hd + 1) * HEAD_DIM)
        cst = jnp.broadcast_to(qc_ref[:, cs], (rows, HEAD_DIM))
        o_ref[:, hd * LANES:(hd + 1) * LANES] = jnp.concatenate([res[:, cs].astype(BF16), cst], axis=1)


def _project_q(h, w_q, q_cols):
    m, k = h.shape
    return pl.pallas_call(
        _q_body,
        grid=(m // ROW_TILE,),
        in_specs=[_row_tile_spec(k), _resident_spec(w_q.shape), _resident_spec(q_cols.shape)],
        out_specs=_row_tile_spec(A_HEADS * LANES),
        out_shape=jax.ShapeDtypeStruct((m, A_HEADS * LANES), BF16),
        compiler_params=_cparams(1),
    )(h, w_q, q_cols)


def _kvsw_body(seq, h_ref, w_ref, ks_ref, vs_ref, kw_ref, vw_ref):
    res = _dot_nt(h_ref[...], w_ref[...])
    rows = res.shape[0]
    tok0 = (pl.program_id(0) * rows) % seq
    pos = tok0 + lax.broadcasted_iota(jnp.int32, (rows, LANES), 0)
    lane = lax.broadcasted_iota(jnp.int32, (rows, LANES), 1)
    onehot = jnp.where((pos >> 6) == lane, 1.0, 0.0).astype(BF16)
    off = (tok0 + lax.broadcasted_iota(jnp.int32, (rows, HEAD_DIM), 0)) & (SEL_TILE - 1)
    lane_h = lax.broadcasted_iota(jnp.int32, (rows, HEAD_DIM), 1)
    pos_cols = jnp.where(lane_h < N_SLOPE_PARTS, off >> POS_SHIFT,
                         jnp.where(lane_h < 2 * N_SLOPE_PARTS, off & (POS_LO - 1), 0)).astype(F32).astype(BF16)
    ones_col = jnp.where(lane_h == 0, 1.0, 0.0).astype(BF16)
    zeros = jnp.zeros((rows, HEAD_DIM), BF16)
    piece = lambda j: res[:, j * HEAD_DIM:(j + 1) * HEAD_DIM].astype(BF16)
    for g in range(A_KV_GROUPS):
        ks_ref[g, :, 0:LANES] = onehot
        ks_ref[g, :, LANES:2 * LANES] = jnp.concatenate([piece(g), pos_cols], axis=1)
        vs_ref[g] = jnp.concatenate([piece(A_KV_GROUPS + g), ones_col], axis=1)
        kw_ref[g] = jnp.concatenate([piece(2 * A_KV_GROUPS + g), zeros], axis=1)
        vw_ref[g] = jnp.concatenate([piece(3 * A_KV_GROUPS + g), ones_col], axis=1)


def _project_kvsw(h, w, b, s):
    m, k = h.shape
    tps = s // ROW_TILE
    assert s // SEL_BLOCK <= LANES, "one-hot selection columns must fit one lane tile"
    spec = lambda width: pl.BlockSpec((None, A_KV_GROUPS, ROW_TILE, width),
                                      lambda i: (i // tps, 0, i % tps, 0))
    shape = lambda width: jax.ShapeDtypeStruct((b, A_KV_GROUPS, s, width), BF16)
    return pl.pallas_call(
        lambda *refs: _kvsw_body(s, *refs),
        grid=(m // ROW_TILE,),
        in_specs=[_row_tile_spec(k), _resident_spec(w.shape)],
        out_specs=[spec(2 * LANES), spec(LANES), spec(LANES), spec(LANES)],
        out_shape=[shape(2 * LANES), shape(LANES), shape(LANES), shape(LANES)],
        compiler_params=_cparams(1),
    )(h, w)


def _kvc_body(h_ref, w_ref, ck_ref, cv_ref, res_sc):
    res = _dot_nt(h_ref[...], w_ref[...])
    n_slabs = res_sc.shape[0]
    for c in range(n_slabs):
        res_sc[c] = res[:, c * LANES:(c + 1) * LANES]
    nrow = res_sc.shape[1] // CMP_STRIDE
    heads_per_slab = LANES // HEAD_DIM
    for c in range(n_slabs):
        out_ref = ck_ref if c < n_slabs // 2 else cv_ref
        g0 = (c % (n_slabs // 2)) * heads_per_slab
        for l in range(CMP_STRIDE):
            slab = res_sc[c, pl.ds(l, nrow, stride=CMP_STRIDE), :]
            for j in range(heads_per_slab):
                out_ref[g0 + j, :, l * HEAD_DIM:(l + 1) * HEAD_DIM] = slab[:, j * HEAD_DIM:(j + 1) * HEAD_DIM]


def _project_kvc(h, w, b, s):
    m, k = h.shape
    tps = s // ROW_TILE
    nrow = ROW_TILE // CMP_STRIDE
    width = CMP_STRIDE * HEAD_DIM
    spec = pl.BlockSpec((None, A_KV_GROUPS, nrow, width), lambda i: (i // tps, 0, i % tps, 0))
    shape = jax.ShapeDtypeStruct((b, A_KV_GROUPS, s // CMP_STRIDE, width), F32)
    return pl.pallas_call(
        _kvc_body,
        grid=(m // ROW_TILE,),
        in_specs=[_row_tile_spec(k), _resident_spec(w.shape)],
        out_specs=[spec, spec],
        out_shape=[shape, shape],
        scratch_shapes=[pltpu.VMEM((w.shape[0] // LANES, ROW_TILE, LANES), F32)],
        compiler_params=_cparams(1),
    )(h, w)


def _compress_one(c, pe, w1, w2p, ncp):
    half = CMP_STRIDE * HEAD_DIM
    hp = lax.Precision.HIGHEST
    h_lo = jnp.dot(c + pe[:, :half], w1[:half, :], preferred_element_type=F32, precision=hp)
    h_hi = jnp.dot(c + pe[:, half:], w1[half:, :], preferred_element_type=F32, precision=hp)
    pre = h_lo + pltpu.roll(h_hi, ncp - 1, 0)
    hid = pre * _sigmoid(pre)
    out = jnp.dot(hid, w2p, preferred_element_type=F32, precision=hp)
    row = lax.broadcasted_iota(jnp.int32, out.shape, 0)
    return jnp.where(row < ncp - 1, out, 0.0)


def _compress_body(ck_ref, cv_ref, pek_ref, w1k_ref, w2k_ref, pev_ref, w1v_ref, w2v_ref,
                   kc_ref, vc_ref):
    ncp = ck_ref.shape[0]
    kc = _compress_one(ck_ref[...], pek_ref[...], w1k_ref[...], w2k_ref[...], ncp)
    vc = _compress_one(cv_ref[...], pev_ref[...], w1v_ref[...], w2v_ref[...], ncp)
    kc_ref[...] = kc.astype(kc_ref.dtype)
    vc_ref[...] = vc.astype(vc_ref.dtype)


def _compress(ck, cv, pe_k, w1_k, w2_k, pe_v, w1_v, w2_v):
    b, g, ncp, cw = ck.shape
    flat = CMP_LEN * HEAD_DIM
    pad2 = lambda w: jnp.pad(w, ((0, 0), (0, LANES - HEAD_DIM)))
    tok = pl.BlockSpec((None, None, ncp, cw), lambda bi, gi: (bi, gi, 0, 0))
    full = lambda shp: pl.BlockSpec(shp, lambda bi, gi: (0,) * len(shp))
    out = pl.BlockSpec((None, None, ncp, LANES), lambda bi, gi: (bi, gi, 0, 0))
    return pl.pallas_call(
        _compress_body,
        grid=(b, g),
        in_specs=[tok, tok,
                  full((1, flat)), full((flat, HEAD_DIM)), full((HEAD_DIM, LANES)),
                  full((1, flat)), full((flat, HEAD_DIM)), full((HEAD_DIM, LANES))],
        out_specs=[out, out],
        out_shape=[jax.ShapeDtypeStruct((b, g, ncp, LANES), BF16)] * 2,
        compiler_params=_cparams(2),
    )(ck, cv,
      pe_k.reshape(1, flat), w1_k.reshape(flat, HEAD_DIM), pad2(w2_k),
      pe_v.reshape(1, flat), w1_v.reshape(flat, HEAD_DIM), pad2(w2_v))


def _nsa_front(n_top, first, qs, kc_ref, vc_ref, ks_ref, vs_ref, kw_ref, vw_ref,
               bw_refs, abc_ref, dmc_ref, tri_ref, slope, ovt_ref, m_sc, acc_sc, qa_sc, used_ref):
    nb = len(qs)
    q0s = [(first + k) * Q_BLOCK for k in range(nb)]
    n_sel = ovt_ref.shape[0]
    rows_of = lambda k: slice(k * QROWS, (k + 1) * QROWS)
    stack = lambda parts: jnp.concatenate(parts, axis=0)

    s_all = _dot_nt(stack(qs), kc_ref[...])
    dmc = dmc_ref[...]
    abc = abc_ref[...]
    sc = stack([jnp.where(dmc <= q0s[k].astype(F32), s_all[rows_of(k)] + abc, NEG) for k in range(nb)])
    mc = jnp.max(sc, axis=1, keepdims=True)
    pc = jnp.exp2(sc - mc)
    lc = jnp.sum(pc, axis=1, keepdims=True)
    pc = pc * jnp.where(mc > 0.5 * NEG, 1.0 / lc, 0.0)
    o_c = jnp.dot(pc.astype(BF16), vc_ref[...], preferred_element_type=F32)[:, :HEAD_DIM]

    head_sum = lambda k: functools.reduce(
        jnp.add, [pc[k * QROWS + r * Q_BLOCK:k * QROWS + (r + 1) * Q_BLOCK] for r in range(A_REP)])
    psum = stack([head_sum(k) for k in range(nb)])
    p1 = psum.astype(BF16)
    r1 = psum - p1.astype(F32)
    p2 = r1.astype(BF16)
    p3 = (r1 - p2.astype(F32)).astype(BF16)
    ovt = ovt_ref[...]
    imp_t = _dot_nt(ovt, p3) + _dot_nt(ovt, p2) + _dot_nt(ovt, p1)

    kb_w = [pl.multiple_of(jnp.maximum(q0 - WINDOW, 0), Q_BLOCK) for q0 in q0s]
    sw = stack([_dot_nt(qs[k], kw_ref[pl.ds(kb_w[k], WIN_KEYS), :]) + bw_refs[k][...] for k in range(nb)])
    pw = jnp.exp2(sw - jnp.max(sw, axis=1, keepdims=True)).astype(BF16)
    accw = stack([jnp.dot(pw[rows_of(k)], vw_ref[pl.ds(kb_w[k], WIN_KEYS), :], preferred_element_type=F32)
                  for k in range(nb)])
    o_w = accw[:, :HEAD_DIM] / accw[:, HEAD_DIM:HEAD_DIM + 1]

    q0a = [pl.multiple_of(q0, Q_BLOCK) for q0 in q0s]
    tri = tri_ref[...]
    lane = lax.broadcasted_iota(jnp.int32, (QROWS, Q_BLOCK), 1)
    k0_cols = ks_ref[0:Q_BLOCK, LANES:2 * LANES]
    s_d = stack([_dot_nt(qs[k], ks_ref[pl.ds(q0a[k], Q_BLOCK), LANES:2 * LANES]) + tri for k in range(nb)])
    s_0 = stack([jnp.where(lane < jnp.minimum(q0s[k], SEL_BLOCK), _dot_nt(qs[k], k0_cols), NEG)
                 for k in range(nb)])
    c_d = stack([slope * ((q0s[k] // SEL_TILE) * SEL_TILE - q0s[k]).astype(F32) for k in range(nb)])
    c_0 = stack([slope * (-q0s[k]).astype(F32) for k in range(nb)])
    m_d = jnp.max(s_d, axis=1, keepdims=True) + c_d
    m_0 = jnp.max(s_0, axis=1, keepdims=True) + c_0
    m_init = jnp.maximum(m_d, m_0)
    p_d = jnp.exp2(s_d - (m_init - c_d)).astype(BF16)
    p_0 = jnp.exp2(s_0 - (m_init - c_0)).astype(BF16)
    for k in range(nb):
        acc_sc[k] = (jnp.dot(p_d[rows_of(k)], vs_ref[pl.ds(q0a[k], Q_BLOCK), :], preferred_element_type=F32)
                     + jnp.dot(p_0[rows_of(k)], vs_ref[0:Q_BLOCK, :], preferred_element_type=F32))
        m_sc[k] = m_init[rows_of(k)]

    blk = lax.broadcasted_iota(jnp.int32, (n_sel, nb * Q_BLOCK), 0)
    lane_q = lax.broadcasted_iota(jnp.int32, (n_sel, nb * Q_BLOCK), 1)
    t_l = q0s[0] + lane_q
    cur = t_l >> 6
    causal_b = blk * SEL_BLOCK <= t_l
    forced = (blk == 0) | (blk == cur) | (blk == cur - 1)
    score = jnp.where(causal_b & jnp.logical_not(forced), imp_t, -1.0)
    lanes_of = lambda k: slice(k * Q_BLOCK, (k + 1) * Q_BLOCK)
    blk_f = lax.broadcasted_iota(jnp.int32, (n_sel, Q_BLOCK), 0).astype(F32)
    scores = [score[:, lanes_of(k)] for k in range(nb)]
    for _ in range(n_top - 3):
        for k in range(nb):
            top = jnp.max(scores[k], axis=0, keepdims=True)
            first_blk = jnp.min(jnp.where(scores[k] == top, blk_f, float(n_sel)), axis=0, keepdims=True)
            scores[k] = jnp.where(blk_f == first_blk, -3e38, scores[k])
    sel = jnp.where(forced | (jnp.concatenate(scores, axis=1) < -1e38), 1.0, 0.0)
    q0_l = t_l & ~(Q_BLOCK - 1)
    in_loop = (sel > 0.0) & (blk * SEL_BLOCK < q0_l) & (blk > 0)
    bias_t = jnp.where(in_loop, 0.0, NEG)
    used = jnp.where(in_loop, 1.0, 0.0)
    blocks_per_tile = SEL_TILE // SEL_BLOCK
    for t in range(used_ref.shape[0]):
        used_ref[t] = jnp.max(used[t * blocks_per_tile:(t + 1) * blocks_per_tile]).astype(jnp.int32)
    bias = bias_t.T.astype(BF16)
    for k in range(nb):
        bias_k = bias[k * Q_BLOCK:(k + 1) * Q_BLOCK]
        qa_sc[k] = jnp.concatenate([jnp.concatenate([bias_k] * A_REP, axis=0), qs[k]], axis=1)
    return o_c, o_w


def _nsa_body(n_top, q_ref, gl_ref, kc_ref, vc_ref, ks_ref, vs_ref, kw_ref, vw_ref, *rest):
    bw_refs = rest[:NSA_BLOCKS]
    (abc_ref, dmc_ref, tri_ref, slope_ref, ovt_ref, gsp_ref, o_ref,
     m_sc, acc_sc, qa_sc, used_ref) = rest[NSA_BLOCKS:]
    first = pl.program_id(2) * NSA_BLOCKS
    slope = slope_ref[...]
    qb = q_ref[...]
    qs = [jnp.concatenate([qb[k * Q_BLOCK:(k + 1) * Q_BLOCK, r * LANES:(r + 1) * LANES]
                           for r in range(A_REP)], axis=0) for k in range(NSA_BLOCKS)]
    o_c, o_w = _nsa_front(n_top, first, qs, kc_ref, vc_ref, ks_ref, vs_ref, kw_ref, vw_ref,
                          bw_refs, abc_ref, dmc_ref, tri_ref, slope, ovt_ref, m_sc, acc_sc, qa_sc, used_ref)

    def sel_tile(tile, carry):
        @pl.when(used_ref[tile] > 0)
        def _():
            kb = pl.multiple_of(tile * SEL_TILE, SEL_TILE)
            scores = [_dot_nt(qa_sc[k], ks_ref[pl.ds(kb, SEL_TILE), :]) for k in range(NSA_BLOCKS)]
            for k in range(NSA_BLOCKS):
                s = scores[k]
                c = slope * (kb - (first + k) * Q_BLOCK).astype(F32)
                mx = s[:, 0:LANES]
                for j in range(1, SEL_CHUNKS):
                    mx = jnp.maximum(mx, s[:, j * LANES:(j + 1) * LANES])
                m_old = m_sc[k]
                m_new = jnp.maximum(m_old, jnp.max(mx, axis=1, keepdims=True) + c)
                shift = m_new - c
                p = jnp.concatenate([jnp.exp2(s[:, j * LANES:(j + 1) * LANES] - shift).astype(BF16)
                                     for j in range(SEL_CHUNKS)], axis=1)
                pv = jnp.dot(p, vs_ref[pl.ds(kb, SEL_TILE), :], preferred_element_type=F32)
                acc_sc[k] = jnp.exp2(m_old - m_new) * acc_sc[k] + pv
                m_sc[k] = m_new
        return carry

    q0_last = (first + NSA_BLOCKS - 1) * Q_BLOCK
    lax.fori_loop(0, (q0_last + SEL_TILE - 1) // SEL_TILE, sel_tile, 0)

    gate = _sigmoid(gl_ref[...])
    g_hi = gate.astype(BF16)
    g_lo = (gate - g_hi.astype(F32)).astype(BF16)
    g_rep = jnp.dot(jnp.concatenate([g_hi, g_lo], axis=1), gsp_ref[...],
                    preferred_element_type=F32)
    for k in range(NSA_BLOCKS):
        acc = acc_sc[k]
        o_s = acc[:, :HEAD_DIM] / acc[:, HEAD_DIM:HEAD_DIM + 1]
        q_rows = slice(k * Q_BLOCK, (k + 1) * Q_BLOCK)
        g_slot = lambda n: g_rep[q_rows, n * LANES:n * LANES + HEAD_DIM]
        for r in range(A_REP):
            rs = slice(r * Q_BLOCK, (r + 1) * Q_BLOCK)
            rk = slice(k * QROWS + r * Q_BLOCK, k * QROWS + (r + 1) * Q_BLOCK)
            o_ref[q_rows, r * HEAD_DIM:(r + 1) * HEAD_DIM] = (
                g_slot(r) * o_c[rk] + g_slot(A_REP + r) * o_s[rs] + g_slot(2 * A_REP + r) * o_w[rk])


def _nsa(q_pad, glog, kc, vc, ks_ext, vs_ext, kw_ext, vw_ext, bw, abc, dmc, tri, slope_b, ovt, gsp):
    b, g, s = ks_ext.shape[:3]
    nq = s // Q_BLOCK
    steps = nq // NSA_BLOCKS
    rows = NSA_BLOCKS * Q_BLOCK
    ncp = kc.shape[2]
    n_top = min(SEL_TOP_N, s // SEL_BLOCK)
    assert n_top >= 3
    n_win_cases = bw.shape[1]
    per_bg = lambda shp: pl.BlockSpec((None, None) + shp, lambda bi, gi, i: (bi, gi, 0, 0))
    per_g = lambda shp: pl.BlockSpec((None,) + shp, lambda bi, gi, i: (gi, 0, 0))
    const = lambda shp: pl.BlockSpec(shp, lambda bi, gi, i: (0, 0))
    win_case = lambda k: pl.BlockSpec(
        (None, None, QROWS, WIN_KEYS),
        lambda bi, gi, i: (gi, jnp.minimum(i * NSA_BLOCKS + k, n_win_cases - 1), 0, 0))
    return pl.pallas_call(
        functools.partial(_nsa_body, n_top),
        grid=(b, g, steps),
        in_specs=[pl.BlockSpec((rows, A_REP * LANES), lambda bi, gi, i: (bi * steps + i, gi)),
                  pl.BlockSpec((rows, LANES), lambda bi, gi, i: (bi * steps + i, gi)),
                  per_bg((ncp, LANES)), per_bg((ncp, LANES)),
                  per_bg((s, ks_ext.shape[3])), per_bg((s, LANES)),
                  per_bg((s, LANES)), per_bg((s, LANES))]
                 + [win_case(k) for k in range(NSA_BLOCKS)]
                 + [per_g((QROWS, ncp)), const((QROWS, ncp)), const((QROWS, Q_BLOCK)),
                    per_g((QROWS, LANES)), const(ovt.shape), const(gsp.shape)],
        out_specs=pl.BlockSpec((None, rows, A_REP * HEAD_DIM), lambda bi, gi, i: (bi, i, gi)),
        out_shape=jax.ShapeDtypeStruct((b, s, A_WIDTH), F32),
        scratch_shapes=[pltpu.VMEM((NSA_BLOCKS, QROWS, LANES), F32),
                        pltpu.VMEM((NSA_BLOCKS, QROWS, LANES), F32),
                        pltpu.VMEM((NSA_BLOCKS, QROWS, 2 * LANES), BF16),
                        pltpu.SMEM((s // SEL_TILE,), jnp.int32)],
        compiler_params=_cparams(3),
    )(q_pad, glog, kc, vc, ks_ext, vs_ext, kw_ext, vw_ext, *([bw] * NSA_BLOCKS),
      abc, dmc, tri, slope_b, ovt, gsp)


def _gelu(x):
    return 0.5 * x * (1.0 + jnp.tanh(np.float32(np.sqrt(2.0 / np.pi)) * (x + 0.044715 * (x * x * x))))


def _sgu_body(uv_ref, zb_ref, lng_ref, lnb_ref, w_ref, bs_ref, o_ref):
    bw = zb_ref.shape[1]
    gd = bw // B_GROUPS
    u = _gelu(uv_ref[:, :bw])
    v = _gelu(uv_ref[:, bw:])
    mu = jnp.mean(v, axis=-1, keepdims=True)
    vc = v - mu
    vn = vc * lax.rsqrt(jnp.mean(vc * vc, axis=-1, keepdims=True) + EPS)
    vn = (vn * lng_ref[...] + lnb_ref[...]).astype(BF16)
    zb = zb_ref[...]
    gate = u * (zb * _sigmoid(zb))
    ti = lax.broadcasted_iota(jnp.int32, (B_CHUNK, B_CHUNK), 0)
    si = lax.broadcasted_iota(jnp.int32, (B_CHUNK, B_CHUNK), 1)
    for gi in range(B_GROUPS):
        wg = jnp.where(si <= ti, w_ref[gi], 0.0).astype(BF16)
        bcol = bs_ref[:, gi:gi + 1]
        for c in range(uv_ref.shape[0] // B_CHUNK):
            rs = slice(c * B_CHUNK, (c + 1) * B_CHUNK)
            cs = slice(gi * gd, (gi + 1) * gd)
            sv = jnp.dot(wg, vn[rs, cs], preferred_element_type=F32) + bcol
            o_ref[rs, cs] = (gate[rs, cs] * sv).astype(o_ref.dtype)


def _sgu(uv, zb, ln_g, ln_b, w_s, b_s):
    m, bw2 = uv.shape
    bw = bw2 // 2
    return pl.pallas_call(
        _sgu_body,
        grid=(m // ROW_TILE,),
        in_specs=[_row_tile_spec(bw2), _row_tile_spec(bw),
                  _resident_spec((1, bw)), _resident_spec((1, bw)),
                  _resident_spec(w_s.shape), _resident_spec((B_CHUNK, B_GROUPS))],
        out_specs=_row_tile_spec(bw),
        out_shape=jax.ShapeDtypeStruct((m, bw), BF16),
        compiler_params=_cparams(1),
    )(uv, zb, ln_g.reshape(1, bw), ln_b.reshape(1, bw), w_s, b_s.T)


def _merge_body(oa_ref, za_ref, ob_ref, ml_ref, wa_ref, wb_ref, o_ref):
    d = o_ref.shape[1]
    za = za_ref[...]
    oa = (oa_ref[...] * (za * _sigmoid(za))).astype(BF16)
    ua = jnp.dot(oa, wa_ref[...], preferred_element_type=F32)
    ub = jnp.dot(ob_ref[...], wb_ref[...], preferred_element_type=F32)
    o_ref[...] = (_sigmoid(ml_ref[:, :d]) * ua + _sigmoid(ml_ref[:, d:]) * ub).astype(o_ref.dtype)


def _merge(oa, za, ob, ml, w_up_a, w_up_b):
    m, aw = oa.shape
    bw = ob.shape[1]
    d = w_up_a.shape[1]
    tm = ROW_TILE
    row = lambda w: pl.BlockSpec((tm, w), lambda i: (i, 0))
    return pl.pallas_call(
        _merge_body,
        grid=(m // tm,),
        in_specs=[row(aw), row(aw), row(bw), row(2 * d),
                  _resident_spec((aw, d)), _resident_spec((bw, d))],
        out_specs=row(d),
        out_shape=jax.ShapeDtypeStruct((m, d), BF16),
        compiler_params=_cparams(1),
    )(oa, za, ob, ml, w_up_a, w_up_b)


def _out_body(x_ref, mg_ref, p_ref, wo_ref, wg_ref, wp_ref, fg_ref, o_ref):
    x1 = x_ref[...] + jnp.dot(mg_ref[...], wo_ref[...], preferred_element_type=F32)
    gl = jnp.dot(x1.astype(BF16), wg_ref[...], preferred_element_type=F32)
    ple = jnp.dot(p_ref[...].astype(BF16), wp_ref[...], preferred_element_type=F32)
    x2 = x1 + _sigmoid(gl) * ple
    y = x2 * lax.rsqrt(jnp.mean(x2 * x2, axis=-1, keepdims=True) + EPS)
    o_ref[...] = y * fg_ref[...]


def _out(x2d, merged, p2d, w_out, w_gate, w_ple, final_g):
    m, d = x2d.shape
    pd = p2d.shape[1]
    tm = ROW_TILE
    row = lambda w: pl.BlockSpec((tm, w), lambda i: (i, 0))
    return pl.pallas_call(
        _out_body,
        grid=(m // tm,),
        in_specs=[row(d), row(d), row(pd),
                  _resident_spec((d, d)), _resident_spec((d, d)), _resident_spec((pd, d)),
                  _resident_spec((1, d))],
        out_specs=row(d),
        out_shape=jax.ShapeDtypeStruct((m, d), F32),
        compiler_params=_cparams(1),
    )(x2d, merged, p2d, w_out, w_gate, w_ple, final_g.reshape(1, d))


def _bf16_parts(x, n):
    parts, rest = [], jnp.asarray(x, F32)
    for _ in range(n):
        piece = rest.astype(BF16)
        parts.append(piece)
        rest = rest - piece.astype(F32)
    return parts


def _attention_constants(s, ncp):
    hh = np.arange(1, A_HEADS + 1, dtype=np.float32)
    slopes = np.power(np.float32(2.0), -8.0 * hh / A_HEADS).astype(np.float32)
    slopes2 = (slopes * np.float32(LOG2E)).astype(np.float32)
    slope_col = jnp.asarray(np.repeat(slopes2.reshape(A_KV_GROUPS, A_REP), Q_BLOCK, axis=1)[:, :, None])
    slope_b = jnp.broadcast_to(slope_col, (A_KV_GROUPS, QROWS, LANES))
    qi = (jnp.arange(QROWS, dtype=jnp.int32) % Q_BLOCK).astype(F32)
    dm = jnp.arange(WIN_KEYS, dtype=F32)[None, :] - qi[:, None]
    offs = jnp.arange(WINDOW // Q_BLOCK + 1, dtype=F32)[:, None, None] * Q_BLOCK
    vis = (dm[None] <= offs) & (dm[None] > offs - WINDOW)
    bw = jnp.where(vis[None], (slope_col * dm[None])[:, None], NEG)
    tri = jnp.where(dm[:, :Q_BLOCK] <= 0, 0.0, NEG)
    c_end = jnp.arange(ncp, dtype=F32) * CMP_STRIDE + (CMP_LEN - 1)
    dmc = c_end[None, :] - qi[:, None]
    abc = slope_col * dmc[None]
    ci = np.arange(ncp)[None, :]
    sj = np.arange(LANES)[:, None]
    ovt = ((CMP_STRIDE * ci < SEL_BLOCK * sj + SEL_BLOCK) &
           (CMP_STRIDE * ci + CMP_LEN > SEL_BLOCK * sj) &
           (ci < ncp - 1)).astype(np.float32)
    parts = _bf16_parts(slopes2, N_SLOPE_PARTS)
    cols = jnp.stack([pp * POS_LO for pp in parts] + parts, axis=1)
    q_cols = jnp.pad(cols, ((0, 0), (0, HEAD_DIM - cols.shape[1]))).reshape(1, A_HEADS * HEAD_DIM)
    n_gates = 3 * A_REP
    gsp = np.zeros((2 * LANES, n_gates * LANES), np.float32)
    for k in range(n_gates):
        gsp[k, k * LANES:k * LANES + HEAD_DIM] = 1.0
        gsp[LANES + k, k * LANES:k * LANES + HEAD_DIM] = 1.0
    return bw, abc, dmc, tri, slope_b, jnp.asarray(ovt, dtype=BF16), q_cols, jnp.asarray(gsp, dtype=BF16)


def kernel(x, p, norm_g, w_in, cmp_pe_k, cmp_w1_k, cmp_w2_k, cmp_pe_v, cmp_w1_v, cmp_w2_v, ln_v_g, ln_v_b, sgu_w, sgu_b, w_up_a, w_up_b, w_out, w_ple, w_ple_gate, final_g):
    b, s, d = x.shape
    assert p.shape[0] == 1, "the final norm is fused into the (single) layer's output kernel"
    m = b * s
    g, r = A_KV_GROUPS, A_REP
    ncp = s // CMP_STRIDE
    b_width = w_up_b.shape[1]
    x2d = x.reshape(m, d)
    w_t = jnp.swapaxes(w_in[0], 0, 1)

    o_kv = A_WIDTH
    o_gl = o_kv + 6 * A_KV_WIDTH
    o_za = o_gl + 3 * A_HEADS
    o_uv = o_za + A_WIDTH
    o_zb = o_uv + 2 * b_width
    o_ml = o_zb + b_width
    wcast = lambda lo, hi: w_t[lo:hi].astype(BF16)
    w_q = (w_t[:o_kv] * np.float32(HEAD_DIM ** -0.5 * LOG2E)).astype(BF16)
    w_gl = w_t[o_gl:o_za].reshape(3, g, r, d).transpose(1, 0, 2, 3).reshape(g, 3 * r, d)
    w_gl = jnp.pad(w_gl, ((0, 0), (0, LANES - 3 * r), (0, 0))).reshape(g * LANES, d).astype(BF16)

    bw, abc, dmc, tri, slope_b, ovt, q_cols, gsp = _attention_constants(s, ncp)

    h = _rmsnorm(x2d, norm_g[0])
    q_pad = _project_q(h, w_q, q_cols)
    ck, cv = _project_kvc(h, wcast(o_kv, o_kv + 2 * A_KV_WIDTH), b, s)
    ks_ext, vs_ext, kw_ext, vw_ext = _project_kvsw(h, wcast(o_kv + 2 * A_KV_WIDTH, o_gl), b, s)
    glog = _project(h, w_gl, F32)
    za = _project(h, wcast(o_za, o_uv), F32)
    uv = _project(h, wcast(o_uv, o_zb), F32)
    zb = _project(h, wcast(o_zb, o_ml), F32)
    ml = _project(h, wcast(o_ml, w_t.shape[0]), F32)

    kc, vc = _compress(ck, cv, cmp_pe_k[0], cmp_w1_k[0], cmp_w2_k[0], cmp_pe_v[0], cmp_w1_v[0], cmp_w2_v[0])
    o_att = _nsa(q_pad, glog, kc, vc, ks_ext, vs_ext, kw_ext, vw_ext, bw, abc, dmc, tri, slope_b, ovt, gsp)

    ob = _sgu(uv, zb, ln_v_g[0], ln_v_b[0], sgu_w[0], sgu_b[0])
    merged = _merge(o_att.reshape(m, A_WIDTH), za, ob, ml, w_up_a[0].astype(BF16), w_up_b[0].astype(BF16))
    out = _out(x2d, merged, p[0].reshape(m, -1), w_out[0].astype(BF16), w_ple_gate[0].astype(BF16),
               w_ple[0].astype(BF16), final_g)
    return out.reshape(b, s, d)
```

```python
import functools

import numpy as np
import jax
import jax.numpy as jnp
from jax import lax
from jax.experimental import pallas as pl
from jax.experimental.pallas import tpu as pltpu

A_HEADS = 16
A_KV_GROUPS = 4
A_REP = A_HEADS // A_KV_GROUPS
HEAD_DIM = 64
A_WIDTH = A_HEADS * HEAD_DIM
A_KV_WIDTH = A_KV_GROUPS * HEAD_DIM
CMP_LEN = 32
CMP_STRIDE = 16
SEL_BLOCK = 64
SEL_TOP_N = 16
WINDOW = 512
Q_BLOCK = 128
FORCE_BONUS = 1000.0
B_GROUPS = 8
B_CHUNK = 128
NEG = -1e30
EPS = 1e-6
LOG2E = float(np.log2(np.e))

LANES = 128
QROWS = A_REP * Q_BLOCK
NSA_BLOCKS = 2
SEL_TILE = 1024
SEL_CHUNKS = SEL_TILE // LANES
WIN_KEYS = WINDOW + Q_BLOCK
ROW_TILE = 512
PROJ_TILE = 1024
VMEM_LIMIT = 56 * 1024 * 1024
POS_SHIFT = 5
POS_LO = 1 << POS_SHIFT
N_SLOPE_PARTS = 3

F32 = jnp.float32
BF16 = jnp.bfloat16


def _cparams(n_axes):
    return pltpu.CompilerParams(dimension_semantics=("arbitrary",) * n_axes,
                                vmem_limit_bytes=VMEM_LIMIT)


def _sigmoid(x):
    return 1.0 / (1.0 + jnp.exp(-x))


def _dot_nt(a, b):
    return lax.dot_general(a, b, (((1,), (1,)), ((), ())), preferred_element_type=F32)


def _row_tile_spec(width):
    return pl.BlockSpec((ROW_TILE, width), lambda i: (i, 0))


def _resident_spec(shape):
    return pl.BlockSpec(shape, lambda i: (0,) * len(shape), pipeline_mode=pl.Buffered(1))


def _rmsnorm_rows(x, g):
    y = x * lax.rsqrt(jnp.mean(x * x, axis=-1, keepdims=True) + EPS)
    return (y * g).astype(BF16)


def _mm_body(h_ref, w_ref, o_ref):
    o_ref[...] = _dot_nt(h_ref[...], w_ref[...]).astype(o_ref.dtype)


def _project(h, w, out_dtype):
    m, k = h.shape
    n = w.shape[0]
    tn = min(n, 2048)
    return pl.pallas_call(
        _mm_body,
        grid=(n // tn, m // PROJ_TILE),
        in_specs=[pl.BlockSpec((PROJ_TILE, k), lambda j, i: (i, 0)),
                  pl.BlockSpec((tn, k), lambda j, i: (j, 0))],
        out_specs=pl.BlockSpec((PROJ_TILE, tn), lambda j, i: (i, j)),
        out_shape=jax.ShapeDtypeStruct((m, n), out_dtype),
        compiler_params=_cparams(2),
    )(h, w)


def _q_body(x_ref, g_ref, w_ref, qc_ref, h_ref, o_ref):
    h = _rmsnorm_rows(x_ref[...], g_ref[...])
    h_ref[...] = h
    res = _dot_nt(h, w_ref[...])
    rows = res.shape[0]
    for hd in range(A_HEADS):
        cs = slice(hd * HEAD_DIM, (hd + 1) * HEAD_DIM)
        cst = jnp.broadcast_to(qc_ref[:, cs], (rows, HEAD_DIM))
        o_ref[:, hd * LANES:(hd + 1) * LANES] = jnp.concatenate([res[:, cs].astype(BF16), cst], axis=1)


def _norm_project_q(x2d, norm_g, w_q, q_cols):
    m, k = x2d.shape
    return pl.pallas_call(
        _q_body,
        grid=(m // ROW_TILE,),
        in_specs=[_row_tile_spec(k), _resident_spec((1, k)), _resident_spec(w_q.shape),
                  _resident_spec(q_cols.shape)],
        out_specs=[_row_tile_spec(k), _row_tile_spec(A_HEADS * LANES)],
        out_shape=[jax.ShapeDtypeStruct((m, k), BF16), jax.ShapeDtypeStruct((m, A_HEADS * LANES), BF16)],
        compiler_params=_cparams(1),
    )(x2d, norm_g.reshape(1, k), w_q, q_cols)


def _kvsw_body(seq, h_ref, w_ref, ks_ref, vs_ref, kw_ref, vw_ref):
    res = _dot_nt(h_ref[...], w_ref[...])
    rows = res.shape[0]
    tok0 = (pl.program_id(0) * rows) % seq
    pos = tok0 + lax.broadcasted_iota(jnp.int32, (rows, LANES), 0)
    lane = lax.broadcasted_iota(jnp.int32, (rows, LANES), 1)
    onehot = jnp.where((pos >> 6) == lane, 1.0, 0.0).astype(BF16)
    off = (tok0 + lax.broadcasted_iota(jnp.int32, (rows, HEAD_DIM), 0)) & (SEL_TILE - 1)
    lane_h = lax.broadcasted_iota(jnp.int32, (rows, HEAD_DIM), 1)
    pos_cols = jnp.where(lane_h < N_SLOPE_PARTS, off >> POS_SHIFT,
                         jnp.where(lane_h < 2 * N_SLOPE_PARTS, off & (POS_LO - 1), 0)).astype(F32).astype(BF16)
    ones_col = jnp.where(lane_h == 0, 1.0, 0.0).astype(BF16)
    zeros = jnp.zeros((rows, HEAD_DIM), BF16)
    piece = lambda j: res[:, j * HEAD_DIM:(j + 1) * HEAD_DIM].astype(BF16)
    for g in range(A_KV_GROUPS):
        ks_ref[g, :, 0:LANES] = onehot
        ks_ref[g, :, LANES:2 * LANES] = jnp.concatenate([piece(g), pos_cols], axis=1)
        vs_ref[g] = jnp.concatenate([piece(A_KV_GROUPS + g), ones_col], axis=1)
        kw_ref[g] = jnp.concatenate([piece(2 * A_KV_GROUPS + g), zeros], axis=1)
        vw_ref[g] = jnp.concatenate([piece(3 * A_KV_GROUPS + g), ones_col], axis=1)


def _project_kvsw(h, w, b, s):
    m, k = h.shape
    tps = s // ROW_TILE
    assert s // SEL_BLOCK <= LANES, "one-hot selection columns must fit one lane tile"
    spec = lambda width: pl.BlockSpec((None, A_KV_GROUPS, ROW_TILE, width),
                                      lambda i: (i // tps, 0, i % tps, 0))
    shape = lambda width: jax.ShapeDtypeStruct((b, A_KV_GROUPS, s, width), BF16)
    return pl.pallas_call(
        lambda *refs: _kvsw_body(s, *refs),
        grid=(m // ROW_TILE,),
        in_specs=[_row_tile_spec(k), _resident_spec(w.shape)],
        out_specs=[spec(2 * LANES), spec(LANES), spec(LANES), spec(LANES)],
        out_shape=[shape(2 * LANES), shape(LANES), shape(LANES), shape(LANES)],
        compiler_params=_cparams(1),
    )(h, w)


def _kvc_body(h_ref, w_ref, ck_ref, cv_ref, res_sc):
    res = _dot_nt(h_ref[...], w_ref[...])
    n_slabs = res_sc.shape[0]
    for c in range(n_slabs):
        res_sc[c] = res[:, c * LANES:(c + 1) * LANES]
    nrow = res_sc.shape[1] // CMP_STRIDE
    heads_per_slab = LANES // HEAD_DIM
    for c in range(n_slabs):
        out_ref = ck_ref if c < n_slabs // 2 else cv_ref
        g0 = (c % (n_slabs // 2)) * heads_per_slab
        for l in range(CMP_STRIDE):
            slab = res_sc[c, pl.ds(l, nrow, stride=CMP_STRIDE), :]
            for j in range(heads_per_slab):
                out_ref[g0 + j, :, l * HEAD_DIM:(l + 1) * HEAD_DIM] = slab[:, j * HEAD_DIM:(j + 1) * HEAD_DIM]


def _project_kvc(h, w, b, s):
    m, k = h.shape
    tps = s // ROW_TILE
    nrow = ROW_TILE // CMP_STRIDE
    width = CMP_STRIDE * HEAD_DIM
    spec = pl.BlockSpec((None, A_KV_GROUPS, nrow, width), lambda i: (i // tps, 0, i % tps, 0))
    shape = jax.ShapeDtypeStruct((b, A_KV_GROUPS, s // CMP_STRIDE, width), F32)
    return pl.pallas_call(
        _kvc_body,
        grid=(m // ROW_TILE,),
        in_specs=[_row_tile_spec(k), _resident_spec(w.shape)],
        out_specs=[spec, spec],
        out_shape=[shape, shape],
        scratch_shapes=[pltpu.VMEM((w.shape[0] // LANES, ROW_TILE, LANES), F32)],
        compiler_params=_cparams(1),
    )(h, w)


def _dot_split(a, b):
    a_hi = a.astype(BF16)
    a_lo = (a - a_hi.astype(F32)).astype(BF16)
    b_hi = b.astype(BF16)
    b_lo = (b - b_hi.astype(F32)).astype(BF16)
    dot = lambda x, y: jnp.dot(x, y, preferred_element_type=F32)
    return dot(a_lo, b_hi) + dot(a_hi, b_lo) + dot(a_hi, b_hi)


def _compress_one(c, pe_ab, w1ab, w2p, ncp):
    h = _dot_split(c, w1ab) + jnp.sum(w1ab * pe_ab, axis=0, keepdims=True)
    pre = h[:, :HEAD_DIM] + pltpu.roll(h[:, HEAD_DIM:], ncp - 1, 0)
    hid = pre * _sigmoid(pre)
    out = _dot_split(hid, w2p)
    row = lax.broadcasted_iota(jnp.int32, out.shape, 0)
    return jnp.where(row < ncp - 1, out, 0.0)


def _compress_body(ck_ref, cv_ref, pek_ref, w1k_ref, w2k_ref, pev_ref, w1v_ref, w2v_ref,
                   kc_ref, vc_ref):
    ncp = ck_ref.shape[0]
    kc = _compress_one(ck_ref[...], pek_ref[...], w1k_ref[...], w2k_ref[...], ncp)
    vc = _compress_one(cv_ref[...], pev_ref[...], w1v_ref[...], w2v_ref[...], ncp)
    c_end = lax.broadcasted_iota(jnp.int32, kc.shape, 0) * CMP_STRIDE + (CMP_LEN - 1)
    col = lax.broadcasted_iota(jnp.int32, kc.shape, 1) - HEAD_DIM
    digits = jnp.where((col >= 0) & (col < N_SLOPE_PARTS), c_end >> POS_SHIFT,
                       jnp.where((col >= N_SLOPE_PARTS) & (col < 2 * N_SLOPE_PARTS), c_end & (POS_LO - 1), 0))
    kc = kc + digits.astype(F32)
    kc_ref[...] = kc.astype(kc_ref.dtype)
    vc_ref[...] = vc.astype(vc_ref.dtype)


def _compress(ck, cv, pe_k, w1_k, w2_k, pe_v, w1_v, w2_v):
    b, g, ncp, cw = ck.shape
    assert CMP_LEN == 2 * CMP_STRIDE and cw == CMP_STRIDE * HEAD_DIM
    pad2 = lambda w: jnp.pad(w, ((0, 0), (0, LANES - HEAD_DIM)))
    w1ab = lambda w1: jnp.concatenate(list(w1.reshape(2, cw, HEAD_DIM)), axis=1)
    pe_ab = lambda pe: jnp.repeat(pe.reshape(2, cw).T, HEAD_DIM, axis=1)
    tok = pl.BlockSpec((None, None, ncp, cw), lambda bi, gi: (bi, gi, 0, 0))
    full = lambda shp: pl.BlockSpec(shp, lambda bi, gi: (0,) * len(shp))
    out = pl.BlockSpec((None, None, ncp, LANES), lambda bi, gi: (bi, gi, 0, 0))
    return pl.pallas_call(
        _compress_body,
        grid=(b, g),
        in_specs=[tok, tok,
                  full((cw, LANES)), full((cw, LANES)), full((HEAD_DIM, LANES)),
                  full((cw, LANES)), full((cw, LANES)), full((HEAD_DIM, LANES))],
        out_specs=[out, out],
        out_shape=[jax.ShapeDtypeStruct((b, g, ncp, LANES), BF16)] * 2,
        compiler_params=_cparams(2),
    )(ck, cv, pe_ab(pe_k), w1ab(w1_k), pad2(w2_k), pe_ab(pe_v), w1ab(w1_v), pad2(w2_v))


def _nsa_front(n_top, first, qs, kc_ref, vc_ref, ks_ref, vs_ref, kw_ref, vw_ref,
               bw_refs, dmc_ref, tri_ref, slope, ovt_ref, m_sc, acc_sc, qa_sc, used_ref):
    nb = len(qs)
    q0s = [(first + k) * Q_BLOCK for k in range(nb)]
    n_sel = ovt_ref.shape[0]
    rows_of = lambda k: slice(k * QROWS, (k + 1) * QROWS)
    stack = lambda parts: jnp.concatenate(parts, axis=0)

    s_all = _dot_nt(stack(qs), kc_ref[...])
    dmc = dmc_ref[...]
    sc = stack([jnp.where(dmc <= q0s[k].astype(F32), s_all[rows_of(k)], NEG) for k in range(nb)])
    mc = jnp.max(sc, axis=1, keepdims=True)
    pc = jnp.exp2(sc - mc)
    lc = jnp.sum(pc, axis=1, keepdims=True)
    pc = pc * jnp.where(mc > 0.5 * NEG, 1.0 / lc, 0.0)
    o_c = jnp.dot(pc.astype(BF16), vc_ref[...], preferred_element_type=F32)[:, :HEAD_DIM]

    head_sum = lambda k: functools.reduce(
        jnp.add, [pc[k * QROWS + r * Q_BLOCK:k * QROWS + (r + 1) * Q_BLOCK] for r in range(A_REP)])
    psum = stack([head_sum(k) for k in range(nb)])
    p1 = psum.astype(BF16)
    r1 = psum - p1.astype(F32)
    p2 = r1.astype(BF16)
    p3 = (r1 - p2.astype(F32)).astype(BF16)
    ovt = ovt_ref[...]
    imp_t = _dot_nt(ovt, p3) + _dot_nt(ovt, p2) + _dot_nt(ovt, p1)

    kb_w = [pl.multiple_of(jnp.maximum(q0 - WINDOW, 0), Q_BLOCK) for q0 in q0s]
    sw = stack([_dot_nt(qs[k], kw_ref[pl.ds(kb_w[k], WIN_KEYS), :]) + bw_refs[k][...] for k in range(nb)])
    pw = jnp.exp2(sw - jnp.max(sw, axis=1, keepdims=True)).astype(BF16)
    accw = stack([jnp.dot(pw[rows_of(k)], vw_ref[pl.ds(kb_w[k], WIN_KEYS), :], preferred_element_type=F32)
                  for k in range(nb)])
    o_w = accw[:, :HEAD_DIM] / accw[:, HEAD_DIM:HEAD_DIM + 1]

    q0a = [pl.multiple_of(q0, Q_BLOCK) for q0 in q0s]
    tri = tri_ref[...]
    lane = lax.broadcasted_iota(jnp.int32, (QROWS, Q_BLOCK), 1)
    k0_cols = ks_ref[0:Q_BLOCK, LANES:2 * LANES]
    s_d = stack([_dot_nt(qs[k], ks_ref[pl.ds(q0a[k], Q_BLOCK), LANES:2 * LANES]) + tri for k in range(nb)])
    s_0 = stack([jnp.where(lane < jnp.minimum(q0s[k], SEL_BLOCK), _dot_nt(qs[k], k0_cols), NEG)
                 for k in range(nb)])
    c_d = stack([slope * ((q0s[k] // SEL_TILE) * SEL_TILE - q0s[k]).astype(F32) for k in range(nb)])
    c_0 = stack([slope * (-q0s[k]).astype(F32) for k in range(nb)])
    m_d = jnp.max(s_d, axis=1, keepdims=True) + c_d
    m_0 = jnp.max(s_0, axis=1, keepdims=True) + c_0
    m_init = jnp.maximum(m_d, m_0)
    p_d = jnp.exp2(s_d - (m_init - c_d)).astype(BF16)
    p_0 = jnp.exp2(s_0 - (m_init - c_0)).astype(BF16)
    for k in range(nb):
        acc_sc[k] = (jnp.dot(p_d[rows_of(k)], vs_ref[pl.ds(q0a[k], Q_BLOCK), :], preferred_element_type=F32)
                     + jnp.dot(p_0[rows_of(k)], vs_ref[0:Q_BLOCK, :], preferred_element_type=F32))
        m_sc[k] = m_init[rows_of(k)]

    blk = lax.broadcasted_iota(jnp.int32, (n_sel, nb * Q_BLOCK), 0)
    lane_q = lax.broadcasted_iota(jnp.int32, (n_sel, nb * Q_BLOCK), 1)
    t_l = q0s[0] + lane_q
    cur = t_l >> 6
    causal_b = blk * SEL_BLOCK <= t_l
    forced = (blk == 0) | (blk == cur) | (blk == cur - 1)
    score = jnp.where(causal_b & jnp.logical_not(forced), imp_t, -1.0)
    lanes_of = lambda k: slice(k * Q_BLOCK, (k + 1) * Q_BLOCK)
    blk_f = lax.broadcasted_iota(jnp.int32, (n_sel, Q_BLOCK), 0).astype(F32)
    scores = [score[:, lanes_of(k)] for k in range(nb)]
    for _ in range(n_top - 3):
        for k in range(nb):
            top = jnp.max(scores[k], axis=0, keepdims=True)
            first_blk = jnp.min(jnp.where(scores[k] == top, blk_f, float(n_sel)), axis=0, keepdims=True)
            scores[k] = jnp.where(blk_f == first_blk, -3e38, scores[k])
    sel = jnp.where(forced | (jnp.concatenate(scores, axis=1) < -1e38), 1.0, 0.0)
    q0_l = t_l & ~(Q_BLOCK - 1)
    in_loop = (sel > 0.0) & (blk * SEL_BLOCK < q0_l) & (blk > 0)
    bias_t = jnp.where(in_loop, 0.0, NEG)
    used = jnp.where(in_loop, 1.0, 0.0)
    blocks_per_tile = SEL_TILE // SEL_BLOCK
    for t in range(used_ref.shape[0]):
        used_ref[t] = jnp.max(used[t * blocks_per_tile:(t + 1) * blocks_per_tile]).astype(jnp.int32)
    bias = bias_t.T.astype(BF16)
    for k in range(nb):
        bias_k = bias[k * Q_BLOCK:(k + 1) * Q_BLOCK]
        qa_sc[k] = jnp.concatenate([jnp.concatenate([bias_k] * A_REP, axis=0), qs[k]], axis=1)
    return o_c, o_w


def _nsa_body(n_top, q_ref, gl_ref, kc_ref, vc_ref, ks_ref, vs_ref, kw_ref, vw_ref, *rest):
    bw_refs = rest[:NSA_BLOCKS]
    (dmc_ref, tri_ref, slope_ref, ovt_ref, gsp_ref, o_ref,
     m_sc, acc_sc, qa_sc, used_ref) = rest[NSA_BLOCKS:]
    first = pl.program_id(2) * NSA_BLOCKS
    slope = slope_ref[...]
    qb = q_ref[...]
    qs = [jnp.concatenate([qb[k * Q_BLOCK:(k + 1) * Q_BLOCK, r * LANES:(r + 1) * LANES]
                           for r in range(A_REP)], axis=0) for k in range(NSA_BLOCKS)]
    o_c, o_w = _nsa_front(n_top, first, qs, kc_ref, vc_ref, ks_ref, vs_ref, kw_ref, vw_ref,
                          bw_refs, dmc_ref, tri_ref, slope, ovt_ref, m_sc, acc_sc, qa_sc, used_ref)

    def sel_tile(tile, carry):
        @pl.when(used_ref[tile] > 0)
        def _():
            kb = pl.multiple_of(tile * SEL_TILE, SEL_TILE)
            scores = [_dot_nt(qa_sc[k], ks_ref[pl.ds(kb, SEL_TILE), :]) for k in range(NSA_BLOCKS)]
            for k in range(NSA_BLOCKS):
                s = scores[k]
                c = slope * (kb - (first + k) * Q_BLOCK).astype(F32)
                mx = s[:, 0:LANES]
                for j in range(1, SEL_CHUNKS):
                    mx = jnp.maximum(mx, s[:, j * LANES:(j + 1) * LANES])
                m_old = m_sc[k]
                m_new = jnp.maximum(m_old, jnp.max(mx, axis=1, keepdims=True) + c)
                shift = m_new - c
                p = jnp.concatenate([jnp.exp2(s[:, j * LANES:(j + 1) * LANES] - shift).astype(BF16)
                                     for j in range(SEL_CHUNKS)], axis=1)
                pv = jnp.dot(p, vs_ref[pl.ds(kb, SEL_TILE), :], preferred_element_type=F32)
                acc_sc[k] = jnp.exp2(m_old - m_new) * acc_sc[k] + pv
                m_sc[k] = m_new
        return carry

    q0_last = (first + NSA_BLOCKS - 1) * Q_BLOCK
    lax.fori_loop(0, (q0_last + SEL_TILE - 1) // SEL_TILE, sel_tile, 0)

    gate = _sigmoid(gl_ref[...])
    g_hi = gate.astype(BF16)
    g_lo = (gate - g_hi.astype(F32)).astype(BF16)
    g_rep = jnp.dot(jnp.concatenate([g_hi, g_lo], axis=1), gsp_ref[...],
                    preferred_element_type=F32)
    for k in range(NSA_BLOCKS):
        acc = acc_sc[k]
        o_s = acc[:, :HEAD_DIM] / acc[:, HEAD_DIM:HEAD_DIM + 1]
        q_rows = slice(k * Q_BLOCK, (k + 1) * Q_BLOCK)
        g_slot = lambda n: g_rep[q_rows, n * LANES:n * LANES + HEAD_DIM]
        for r in range(A_REP):
            rs = slice(r * Q_BLOCK, (r + 1) * Q_BLOCK)
            rk = slice(k * QROWS + r * Q_BLOCK, k * QROWS + (r + 1) * Q_BLOCK)
            o_ref[q_rows, r * HEAD_DIM:(r + 1) * HEAD_DIM] = (
                g_slot(r) * o_c[rk] + g_slot(A_REP + r) * o_s[rs] + g_slot(2 * A_REP + r) * o_w[rk])


def _nsa(q_pad, glog, kc, vc, ks_ext, vs_ext, kw_ext, vw_ext, bw, dmc, tri, slope_b, ovt, gsp):
    b, g, s = ks_ext.shape[:3]
    nq = s // Q_BLOCK
    steps = nq // NSA_BLOCKS
    rows = NSA_BLOCKS * Q_BLOCK
    ncp = kc.shape[2]
    n_top = min(SEL_TOP_N, s // SEL_BLOCK)
    assert n_top >= 3
    n_win_cases = bw.shape[1]
    per_bg = lambda shp: pl.BlockSpec((None, None) + shp, lambda bi, gi, i: (bi, gi, 0, 0))
    per_g = lambda shp: pl.BlockSpec((None,) + shp, lambda bi, gi, i: (gi, 0, 0))
    const = lambda shp: pl.BlockSpec(shp, lambda bi, gi, i: (0, 0))
    win_case = lambda k: pl.BlockSpec(
        (None, None, QROWS, WIN_KEYS),
        lambda bi, gi, i: (gi, jnp.minimum(i * NSA_BLOCKS + k, n_win_cases - 1), 0, 0))
    return pl.pallas_call(
        functools.partial(_nsa_body, n_top),
        grid=(b, g, steps),
        in_specs=[pl.BlockSpec((rows, A_REP * LANES), lambda bi, gi, i: (bi * steps + i, gi)),
                  pl.BlockSpec((rows, LANES), lambda bi, gi, i: (bi * steps + i, gi)),
                  per_bg((ncp, LANES)), per_bg((ncp, LANES)),
                  per_bg((s, ks_ext.shape[3])), per_bg((s, LANES)),
                  per_bg((s, LANES)), per_bg((s, LANES))]
                 + [win_case(k) for k in range(NSA_BLOCKS)]
                 + [const((QROWS, ncp)), const((QROWS, Q_BLOCK)),
                    per_g((QROWS, LANES)), const(ovt.shape), const(gsp.shape)],
        out_specs=pl.BlockSpec((None, rows, A_REP * HEAD_DIM), lambda bi, gi, i: (bi, i, gi)),
        out_shape=jax.ShapeDtypeStruct((b, s, A_WIDTH), F32),
        scratch_shapes=[pltpu.VMEM((NSA_BLOCKS, QROWS, LANES), F32),
                        pltpu.VMEM((NSA_BLOCKS, QROWS, LANES), F32),
                        pltpu.VMEM((NSA_BLOCKS, QROWS, 2 * LANES), BF16),
                        pltpu.SMEM((s // SEL_TILE,), jnp.int32)],
        compiler_params=_cparams(3),
    )(q_pad, glog, kc, vc, ks_ext, vs_ext, kw_ext, vw_ext, *([bw] * NSA_BLOCKS),
      dmc, tri, slope_b, ovt, gsp)


def _gelu(x):
    return 0.5 * x * (1.0 + jnp.tanh(np.float32(np.sqrt(2.0 / np.pi)) * (x + 0.044715 * (x * x * x))))


def _sgu_body(uv_ref, zb_ref, lng_ref, lnb_ref, w_ref, bs_ref, o_ref):
    bw = zb_ref.shape[1]
    gd = bw // B_GROUPS
    u = _gelu(uv_ref[:, :bw])
    v = _gelu(uv_ref[:, bw:])
    mu = jnp.mean(v, axis=-1, keepdims=True)
    vc = v - mu
    vn = vc * lax.rsqrt(jnp.mean(vc * vc, axis=-1, keepdims=True) + EPS)
    vn = (vn * lng_ref[...] + lnb_ref[...]).astype(BF16)
    zb = zb_ref[...]
    gate = u * (zb * _sigmoid(zb))
    ti = lax.broadcasted_iota(jnp.int32, (B_CHUNK, B_CHUNK), 0)
    si = lax.broadcasted_iota(jnp.int32, (B_CHUNK, B_CHUNK), 1)
    for gi in range(B_GROUPS):
        wg = jnp.where(si <= ti, w_ref[gi], 0.0).astype(BF16)
        bcol = bs_ref[:, gi:gi + 1]
        for c in range(uv_ref.shape[0] // B_CHUNK):
            rs = slice(c * B_CHUNK, (c + 1) * B_CHUNK)
            cs = slice(gi * gd, (gi + 1) * gd)
            sv = jnp.dot(wg, vn[rs, cs], preferred_element_type=F32) + bcol
            o_ref[rs, cs] = (gate[rs, cs] * sv).astype(o_ref.dtype)


def _sgu(uv, zb, ln_g, ln_b, w_s, b_s):
    m, bw2 = uv.shape
    bw = bw2 // 2
    return pl.pallas_call(
        _sgu_body,
        grid=(m // ROW_TILE,),
        in_specs=[_row_tile_spec(bw2), _row_tile_spec(bw),
                  _resident_spec((1, bw)), _resident_spec((1, bw)),
                  _resident_spec(w_s.shape), _resident_spec((B_CHUNK, B_GROUPS))],
        out_specs=_row_tile_spec(bw),
        out_shape=jax.ShapeDtypeStruct((m, bw), BF16),
        compiler_params=_cparams(1),
    )(uv, zb, ln_g.reshape(1, bw), ln_b.reshape(1, bw), w_s, b_s.T)


def _merge_body(oa_ref, za_ref, ob_ref, ml_ref, wa_ref, wb_ref, o_ref):
    d = o_ref.shape[1]
    za = za_ref[...]
    oa = (oa_ref[...] * (za * _sigmoid(za))).astype(BF16)
    ua = jnp.dot(oa, wa_ref[...], preferred_element_type=F32)
    ub = jnp.dot(ob_ref[...], wb_ref[...], preferred_element_type=F32)
    o_ref[...] = (_sigmoid(ml_ref[:, :d]) * ua + _sigmoid(ml_ref[:, d:]) * ub).astype(o_ref.dtype)


def _merge(oa, za, ob, ml, w_up_a, w_up_b):
    m, aw = oa.shape
    bw = ob.shape[1]
    d = w_up_a.shape[1]
    tm = ROW_TILE
    row = lambda w: pl.BlockSpec((tm, w), lambda i: (i, 0))
    return pl.pallas_call(
        _merge_body,
        grid=(m // tm,),
        in_specs=[row(aw), row(aw), row(bw), row(2 * d),
                  _resident_spec((aw, d)), _resident_spec((bw, d))],
        out_specs=row(d),
        out_shape=jax.ShapeDtypeStruct((m, d), BF16),
        compiler_params=_cparams(1),
    )(oa, za, ob, ml, w_up_a, w_up_b)


def _out_body(x_ref, mg_ref, p_ref, wo_ref, wg_ref, wp_ref, fg_ref, o_ref):
    x1 = x_ref[...] + jnp.dot(mg_ref[...], wo_ref[...], preferred_element_type=F32)
    gl = jnp.dot(x1.astype(BF16), wg_ref[...], preferred_element_type=F32)
    ple = jnp.dot(p_ref[...].astype(BF16), wp_ref[...], preferred_element_type=F32)
    x2 = x1 + _sigmoid(gl) * ple
    y = x2 * lax.rsqrt(jnp.mean(x2 * x2, axis=-1, keepdims=True) + EPS)
    o_ref[...] = y * fg_ref[...]


def _out(x2d, merged, p2d, w_out, w_gate, w_ple, final_g):
    m, d = x2d.shape
    pd = p2d.shape[1]
    tm = ROW_TILE
    row = lambda w: pl.BlockSpec((tm, w), lambda i: (i, 0))
    return pl.pallas_call(
        _out_body,
        grid=(m // tm,),
        in_specs=[row(d), row(d), row(pd),
                  _resident_spec((d, d)), _resident_spec((d, d)), _resident_spec((pd, d)),
                  _resident_spec((1, d))],
        out_specs=row(d),
        out_shape=jax.ShapeDtypeStruct((m, d), F32),
        compiler_params=_cparams(1),
    )(x2d, merged, p2d, w_out, w_gate, w_ple, final_g.reshape(1, d))


def _bf16_parts(x, n):
    parts, rest = [], jnp.asarray(x, F32)
    for _ in range(n):
        piece = rest.astype(BF16)
        parts.append(piece)
        rest = rest - piece.astype(F32)
    return parts


def _attention_constants(s, ncp):
    hh = np.arange(1, A_HEADS + 1, dtype=np.float32)
    slopes = np.power(np.float32(2.0), -8.0 * hh / A_HEADS).astype(np.float32)
    slopes2 = (slopes * np.float32(LOG2E)).astype(np.float32)
    slope_col = jnp.asarray(np.repeat(slopes2.reshape(A_KV_GROUPS, A_REP), Q_BLOCK, axis=1)[:, :, None])
    slope_b = jnp.broadcast_to(slope_col, (A_KV_GROUPS, QROWS, LANES))
    qi = (jnp.arange(QROWS, dtype=jnp.int32) % Q_BLOCK).astype(F32)
    dm = jnp.arange(WIN_KEYS, dtype=F32)[None, :] - qi[:, None]
    offs = jnp.arange(WINDOW // Q_BLOCK + 1, dtype=F32)[:, None, None] * Q_BLOCK
    vis = (dm[None] <= offs) & (dm[None] > offs - WINDOW)
    bw = jnp.where(vis[None], (slope_col * dm[None])[:, None], NEG)
    tri = jnp.where(dm[:, :Q_BLOCK] <= 0, 0.0, NEG)
    c_end = jnp.arange(ncp, dtype=F32) * CMP_STRIDE + (CMP_LEN - 1)
    assert ((ncp - 1) * CMP_STRIDE + CMP_LEN - 1) >> POS_SHIFT <= 256, "ALiBi digits must be exact in bf16"
    dmc = c_end[None, :] - qi[:, None]
    ci = np.arange(ncp)[None, :]
    sj = np.arange(LANES)[:, None]
    ovt = ((CMP_STRIDE * ci < SEL_BLOCK * sj + SEL_BLOCK) &
           (CMP_STRIDE * ci + CMP_LEN > SEL_BLOCK * sj) &
           (ci < ncp - 1)).astype(np.float32)
    parts = _bf16_parts(slopes2, N_SLOPE_PARTS)
    cols = jnp.stack([pp * POS_LO for pp in parts] + parts, axis=1)
    q_cols = jnp.pad(cols, ((0, 0), (0, HEAD_DIM - cols.shape[1]))).reshape(1, A_HEADS * HEAD_DIM)
    n_gates = 3 * A_REP
    gsp = np.zeros((2 * LANES, n_gates * LANES), np.float32)
    for k in range(n_gates):
        gsp[k, k * LANES:k * LANES + HEAD_DIM] = 1.0
        gsp[LANES + k, k * LANES:k * LANES + HEAD_DIM] = 1.0
    return bw, dmc, tri, slope_b, jnp.asarray(ovt, dtype=BF16), q_cols, jnp.asarray(gsp, dtype=BF16)


def kernel(x, p, norm_g, w_in, cmp_pe_k, cmp_w1_k, cmp_w2_k, cmp_pe_v, cmp_w1_v, cmp_w2_v, ln_v_g, ln_v_b, sgu_w, sgu_b, w_up_a, w_up_b, w_out, w_ple, w_ple_gate, final_g):
    b, s, d = x.shape
    assert p.shape[0] == 1, "the final norm is fused into the (single) layer's output kernel"
    m = b * s
    g, r = A_KV_GROUPS, A_REP
    ncp = s // CMP_STRIDE
    b_width = w_up_b.shape[1]
    x2d = x.reshape(m, d)
    w_t = jnp.swapaxes(w_in[0], 0, 1)

    o_kv = A_WIDTH
    o_gl = o_kv + 6 * A_KV_WIDTH
    o_za = o_gl + 3 * A_HEADS
    o_uv = o_za + A_WIDTH
    o_zb = o_uv + 2 * b_width
    o_ml = o_zb + b_width
    wcast = lambda lo, hi: w_t[lo:hi].astype(BF16)
    w_q = (w_t[:o_kv] * np.float32(HEAD_DIM ** -0.5 * LOG2E)).astype(BF16)
    w_gl = w_t[o_gl:o_za].reshape(3, g, r, d).transpose(1, 0, 2, 3).reshape(g, 3 * r, d)
    w_gl = jnp.pad(w_gl, ((0, 0), (0, LANES - 3 * r), (0, 0))).reshape(g * LANES, d).astype(BF16)

    bw, dmc, tri, slope_b, ovt, q_cols, gsp = _attention_constants(s, ncp)

    h, q_pad = _norm_project_q(x2d, norm_g[0], w_q, q_cols)
    ck, cv = _project_kvc(h, wcast(o_kv, o_kv + 2 * A_KV_WIDTH), b, s)
    ks_ext, vs_ext, kw_ext, vw_ext = _project_kvsw(h, wcast(o_kv + 2 * A_KV_WIDTH, o_gl), b, s)
    glog = _project(h, w_gl, F32)
    za = _project(h, wcast(o_za, o_uv), F32)
    uv = _project(h, wcast(o_uv, o_zb), F32)
    zb = _project(h, wcast(o_zb, o_ml), F32)
    ml = _project(h, wcast(o_ml, w_t.shape[0]), F32)

    kc, vc = _compress(ck, cv, cmp_pe_k[0], cmp_w1_k[0], cmp_w2_k[0], cmp_pe_v[0], cmp_w1_v[0], cmp_w2_v[0])
    o_att = _nsa(q_pad, glog, kc, vc, ks_ext, vs_ext, kw_ext, vw_ext, bw, dmc, tri, slope_b, ovt, gsp)

    ob = _sgu(uv, zb, ln_v_g[0], ln_v_b[0], sgu_w[0], sgu_b[0])
    merged = _merge(o_att.reshape(m, A_WIDTH), za, ob, ml, w_up_a[0].astype(BF16), w_up_b[0].astype(BF16))
    out = _out(x2d, merged, p[0].reshape(m, -1), w_out[0].astype(BF16), w_ple_gate[0].astype(BF16),
               w_ple[0].astype(BF16), final_g)
    return out.reshape(b, s, d)
```

```python
import functools

import numpy as np
import jax
import jax.numpy as jnp
from jax import lax
from jax.experimental import pallas as pl
from jax.experimental.pallas import tpu as pltpu

A_HEADS = 16
A_KV_GROUPS = 4
A_REP = A_HEADS // A_KV_GROUPS
HEAD_DIM = 64
A_WIDTH = A_HEADS * HEAD_DIM
A_KV_WIDTH = A_KV_GROUPS * HEAD_DIM
CMP_LEN = 32
CMP_STRIDE = 16
SEL_BLOCK = 64
SEL_TOP_N = 16
WINDOW = 512
Q_BLOCK = 128
FORCE_BONUS = 1000.0
B_GROUPS = 8
B_CHUNK = 128
NEG = -1e30
EPS = 1e-6
LOG2E = float(np.log2(np.e))

LANES = 128
QROWS = A_REP * Q_BLOCK
NSA_BLOCKS = 2
SEL_TILE = 1024
SEL_CHUNKS = SEL_TILE // LANES
WIN_KEYS = WINDOW + Q_BLOCK
ROW_TILE = 512
PROJ_TILE = 1024
VMEM_LIMIT = 56 * 1024 * 1024
POS_SHIFT = 5
POS_LO = 1 << POS_SHIFT
N_SLOPE_PARTS = 3

F32 = jnp.float32
BF16 = jnp.bfloat16


def _cparams(n_axes):
    return pltpu.CompilerParams(dimension_semantics=("arbitrary",) * n_axes,
                                vmem_limit_bytes=VMEM_LIMIT)


def _sigmoid(x):
    return 1.0 / (1.0 + jnp.exp(-x))


def _dot_nt(a, b):
    return lax.dot_general(a, b, (((1,), (1,)), ((), ())), preferred_element_type=F32)


def _row_tile_spec(width):
    return pl.BlockSpec((ROW_TILE, width), lambda i: (i, 0))


def _resident_spec(shape):
    return pl.BlockSpec(shape, lambda i: (0,) * len(shape), pipeline_mode=pl.Buffered(1))


def _rmsnorm_rows(x, g):
    y = x * lax.rsqrt(jnp.mean(x * x, axis=-1, keepdims=True) + EPS)
    return (y * g).astype(BF16)


def _mm_body(h_ref, w_ref, o_ref):
    o_ref[...] = _dot_nt(h_ref[...], w_ref[...]).astype(o_ref.dtype)


def _project(h, w, out_dtype):
    m, k = h.shape
    n = w.shape[0]
    tn = min(n, 2048)
    return pl.pallas_call(
        _mm_body,
        grid=(n // tn, m // PROJ_TILE),
        in_specs=[pl.BlockSpec((PROJ_TILE, k), lambda j, i: (i, 0)),
                  pl.BlockSpec((tn, k), lambda j, i: (j, 0))],
        out_specs=pl.BlockSpec((PROJ_TILE, tn), lambda j, i: (i, j)),
        out_shape=jax.ShapeDtypeStruct((m, n), out_dtype),
        compiler_params=_cparams(2),
    )(h, w)


def _q_body(x_ref, g_ref, w_ref, qc_ref, h_ref, o_ref):
    h = _rmsnorm_rows(x_ref[...], g_ref[...])
    h_ref[...] = h
    res = _dot_nt(h, w_ref[...])
    rows = res.shape[0]
    for hd in range(A_HEADS):
        cs = slice(hd * HEAD_DIM, (hd + 1) * HEAD_DIM)
        cst = jnp.broadcast_to(qc_ref[:, cs], (rows, HEAD_DIM))
        o_ref[:, hd * LANES:(hd + 1) * LANES] = jnp.concatenate([res[:, cs].astype(BF16), cst], axis=1)


def _norm_project_q(x2d, norm_g, w_q, q_cols):
    m, k = x2d.shape
    return pl.pallas_call(
        _q_body,
        grid=(m // ROW_TILE,),
        in_specs=[_row_tile_spec(k), _resident_spec((1, k)), _resident_spec(w_q.shape),
                  _resident_spec(q_cols.shape)],
        out_specs=[_row_tile_spec(k), _row_tile_spec(A_HEADS * LANES)],
        out_shape=[jax.ShapeDtypeStruct((m, k), BF16), jax.ShapeDtypeStruct((m, A_HEADS * LANES), BF16)],
        compiler_params=_cparams(1),
    )(x2d, norm_g.reshape(1, k), w_q, q_cols)


def _kvsw_body(seq, h_ref, w_ref, ks_ref, vs_ref, kw_ref, vw_ref):
    res = _dot_nt(h_ref[...], w_ref[...])
    rows = res.shape[0]
    tok0 = (pl.program_id(0) * rows) % seq
    pos = tok0 + lax.broadcasted_iota(jnp.int32, (rows, LANES), 0)
    lane = lax.broadcasted_iota(jnp.int32, (rows, LANES), 1)
    onehot = jnp.where((pos >> 6) == lane, 1.0, 0.0).astype(BF16)
    off = (tok0 + lax.broadcasted_iota(jnp.int32, (rows, HEAD_DIM), 0)) & (SEL_TILE - 1)
    lane_h = lax.broadcasted_iota(jnp.int32, (rows, HEAD_DIM), 1)
    pos_cols = jnp.where(lane_h < N_SLOPE_PARTS, off >> POS_SHIFT,
                         jnp.where(lane_h < 2 * N_SLOPE_PARTS, off & (POS_LO - 1), 0)).astype(F32).astype(BF16)
    ones_col = jnp.where(lane_h == 0, 1.0, 0.0).astype(BF16)
    zeros = jnp.zeros((rows, HEAD_DIM), BF16)
    piece = lambda j: res[:, j * HEAD_DIM:(j + 1) * HEAD_DIM].astype(BF16)
    for g in range(A_KV_GROUPS):
        ks_ref[g, :, 0:LANES] = onehot
        ks_ref[g, :, LANES:2 * LANES] = jnp.concatenate([piece(g), pos_cols], axis=1)
        vs_ref[g] = jnp.concatenate([piece(A_KV_GROUPS + g), ones_col], axis=1)
        kw_ref[g] = jnp.concatenate([piece(2 * A_KV_GROUPS + g), zeros], axis=1)
        vw_ref[g] = jnp.concatenate([piece(3 * A_KV_GROUPS + g), ones_col], axis=1)


def _project_kvsw(h, w, b, s):
    m, k = h.shape
    tps = s // ROW_TILE
    assert s // SEL_BLOCK <= LANES, "one-hot selection columns must fit one lane tile"
    spec = lambda width: pl.BlockSpec((None, A_KV_GROUPS, ROW_TILE, width),
                                      lambda i: (i // tps, 0, i % tps, 0))
    shape = lambda width: jax.ShapeDtypeStruct((b, A_KV_GROUPS, s, width), BF16)
    return pl.pallas_call(
        lambda *refs: _kvsw_body(s, *refs),
        grid=(m // ROW_TILE,),
        in_specs=[_row_tile_spec(k), _resident_spec(w.shape)],
        out_specs=[spec(2 * LANES), spec(LANES), spec(LANES), spec(LANES)],
        out_shape=[shape(2 * LANES), shape(LANES), shape(LANES), shape(LANES)],
        compiler_params=_cparams(1),
    )(h, w)


def _kvc_body(h_ref, w_ref, ck_ref, cv_ref, res_sc):
    res = _dot_nt(h_ref[...], w_ref[...])
    n_slabs = res_sc.shape[0]
    for c in range(n_slabs):
        res_sc[c] = res[:, c * LANES:(c + 1) * LANES]
    nrow = res_sc.shape[1] // CMP_STRIDE
    heads_per_slab = LANES // HEAD_DIM
    for c in range(n_slabs):
        out_ref = ck_ref if c < n_slabs // 2 else cv_ref
        g0 = (c % (n_slabs // 2)) * heads_per_slab
        for l in range(CMP_STRIDE):
            slab = res_sc[c, pl.ds(l, nrow, stride=CMP_STRIDE), :]
            for j in range(heads_per_slab):
                out_ref[g0 + j, :, l * HEAD_DIM:(l + 1) * HEAD_DIM] = slab[:, j * HEAD_DIM:(j + 1) * HEAD_DIM]


def _project_kvc(h, w, b, s):
    m, k = h.shape
    tps = s // ROW_TILE
    nrow = ROW_TILE // CMP_STRIDE
    width = CMP_STRIDE * HEAD_DIM
    spec = pl.BlockSpec((None, A_KV_GROUPS, nrow, width), lambda i: (i // tps, 0, i % tps, 0))
    shape = jax.ShapeDtypeStruct((b, A_KV_GROUPS, s // CMP_STRIDE, width), F32)
    return pl.pallas_call(
        _kvc_body,
        grid=(m // ROW_TILE,),
        in_specs=[_row_tile_spec(k), _resident_spec(w.shape)],
        out_specs=[spec, spec],
        out_shape=[shape, shape],
        scratch_shapes=[pltpu.VMEM((w.shape[0] // LANES, ROW_TILE, LANES), F32)],
        compiler_params=_cparams(1),
    )(h, w)


def _dot_split(a, b):
    a_hi = a.astype(BF16)
    a_lo = (a - a_hi.astype(F32)).astype(BF16)
    b_hi = b.astype(BF16)
    b_lo = (b - b_hi.astype(F32)).astype(BF16)
    dot = lambda x, y: jnp.dot(x, y, preferred_element_type=F32)
    return dot(a_lo, b_hi) + dot(a_hi, b_lo) + dot(a_hi, b_hi)


def _compress_one(c, pe_ab, w1ab, w2p, ncp):
    h = _dot_split(c, w1ab) + jnp.sum(w1ab * pe_ab, axis=0, keepdims=True)
    pre = h[:, :HEAD_DIM] + pltpu.roll(h[:, HEAD_DIM:], ncp - 1, 0)
    hid = pre * _sigmoid(pre)
    out = _dot_split(hid, w2p)
    row = lax.broadcasted_iota(jnp.int32, out.shape, 0)
    return jnp.where(row < ncp - 1, out, 0.0)


def _compress_body(ck_ref, cv_ref, pek_ref, w1k_ref, w2k_ref, pev_ref, w1v_ref, w2v_ref,
                   kc_ref, vc_ref):
    ncp = ck_ref.shape[0]
    kc = _compress_one(ck_ref[...], pek_ref[...], w1k_ref[...], w2k_ref[...], ncp)
    vc = _compress_one(cv_ref[...], pev_ref[...], w1v_ref[...], w2v_ref[...], ncp)
    c_end = lax.broadcasted_iota(jnp.int32, kc.shape, 0) * CMP_STRIDE + (CMP_LEN - 1)
    col = lax.broadcasted_iota(jnp.int32, kc.shape, 1) - HEAD_DIM
    digits = jnp.where((col >= 0) & (col < N_SLOPE_PARTS), c_end >> POS_SHIFT,
                       jnp.where((col >= N_SLOPE_PARTS) & (col < 2 * N_SLOPE_PARTS), c_end & (POS_LO - 1), 0))
    kc = kc + digits.astype(F32)
    kc_ref[...] = kc.astype(kc_ref.dtype)
    vc_ref[...] = vc.astype(vc_ref.dtype)


def _compress(ck, cv, pe_k, w1_k, w2_k, pe_v, w1_v, w2_v):
    b, g, ncp, cw = ck.shape
    assert CMP_LEN == 2 * CMP_STRIDE and cw == CMP_STRIDE * HEAD_DIM
    pad2 = lambda w: jnp.pad(w, ((0, 0), (0, LANES - HEAD_DIM)))
    w1ab = lambda w1: jnp.concatenate(list(w1.reshape(2, cw, HEAD_DIM)), axis=1)
    pe_ab = lambda pe: jnp.repeat(pe.reshape(2, cw).T, HEAD_DIM, axis=1)
    tok = pl.BlockSpec((None, None, ncp, cw), lambda bi, gi: (bi, gi, 0, 0))
    full = lambda shp: pl.BlockSpec(shp, lambda bi, gi: (0,) * len(shp))
    out = pl.BlockSpec((None, None, ncp, LANES), lambda bi, gi: (bi, gi, 0, 0))
    return pl.pallas_call(
        _compress_body,
        grid=(b, g),
        in_specs=[tok, tok,
                  full((cw, LANES)), full((cw, LANES)), full((HEAD_DIM, LANES)),
                  full((cw, LANES)), full((cw, LANES)), full((HEAD_DIM, LANES))],
        out_specs=[out, out],
        out_shape=[jax.ShapeDtypeStruct((b, g, ncp, LANES), BF16)] * 2,
        compiler_params=_cparams(2),
    )(ck, cv, pe_ab(pe_k), w1ab(w1_k), pad2(w2_k), pe_ab(pe_v), w1ab(w1_v), pad2(w2_v))


def _nsa_front(n_top, first, qs, kc_ref, vc_ref, ks_ref, vs_ref, kw_ref, vw_ref,
               bw_refs, dmc_ref, tri_ref, slope, ovt_ref, m_sc, acc_sc, qa_sc, used_ref):
    nb = len(qs)
    q0s = [(first + k) * Q_BLOCK for k in range(nb)]
    n_sel = ovt_ref.shape[0]
    rows_of = lambda k: slice(k * QROWS, (k + 1) * QROWS)
    stack = lambda parts: jnp.concatenate(parts, axis=0)

    s_all = _dot_nt(stack(qs), kc_ref[...])
    dmc = dmc_ref[...]
    sc = stack([jnp.where(dmc <= q0s[k].astype(F32), s_all[rows_of(k)], NEG) for k in range(nb)])
    mc = jnp.max(sc, axis=1, keepdims=True)
    pc = jnp.exp2(sc - mc)
    lc = jnp.sum(pc, axis=1, keepdims=True)
    pc = pc * jnp.where(mc > 0.5 * NEG, 1.0 / lc, 0.0)
    o_c = jnp.dot(pc.astype(BF16), vc_ref[...], preferred_element_type=F32)[:, :HEAD_DIM]

    head_sum = lambda k: functools.reduce(
        jnp.add, [pc[k * QROWS + r * Q_BLOCK:k * QROWS + (r + 1) * Q_BLOCK] for r in range(A_REP)])
    psum = stack([head_sum(k) for k in range(nb)])
    p1 = psum.astype(BF16)
    r1 = psum - p1.astype(F32)
    p2 = r1.astype(BF16)
    p3 = (r1 - p2.astype(F32)).astype(BF16)
    ovt = ovt_ref[...]
    imp_t = _dot_nt(ovt, p3) + _dot_nt(ovt, p2) + _dot_nt(ovt, p1)

    kb_w = [pl.multiple_of(jnp.maximum(q0 - WINDOW, 0), Q_BLOCK) for q0 in q0s]
    sw = stack([_dot_nt(qs[k], kw_ref[pl.ds(kb_w[k], WIN_KEYS), :]) + bw_refs[k][...] for k in range(nb)])
    pw = jnp.exp2(sw - jnp.max(sw, axis=1, keepdims=True)).astype(BF16)
    accw = stack([jnp.dot(pw[rows_of(k)], vw_ref[pl.ds(kb_w[k], WIN_KEYS), :], preferred_element_type=F32)
                  for k in range(nb)])
    o_w = accw[:, :HEAD_DIM] / accw[:, HEAD_DIM:HEAD_DIM + 1]

    q0a = [pl.multiple_of(q0, Q_BLOCK) for q0 in q0s]
    tri = tri_ref[...]
    lane = lax.broadcasted_iota(jnp.int32, (QROWS, Q_BLOCK), 1)
    k0_cols = ks_ref[0:Q_BLOCK, LANES:2 * LANES]
    s_d = stack([_dot_nt(qs[k], ks_ref[pl.ds(q0a[k], Q_BLOCK), LANES:2 * LANES]) + tri for k in range(nb)])
    s_0 = stack([jnp.where(lane < jnp.minimum(q0s[k], SEL_BLOCK), _dot_nt(qs[k], k0_cols), NEG)
                 for k in range(nb)])
    c_d = stack([slope * ((q0s[k] // SEL_TILE) * SEL_TILE - q0s[k]).astype(F32) for k in range(nb)])
    c_0 = stack([slope * (-q0s[k]).astype(F32) for k in range(nb)])
    m_d = jnp.max(s_d, axis=1, keepdims=True) + c_d
    m_0 = jnp.max(s_0, axis=1, keepdims=True) + c_0
    m_init = jnp.maximum(m_d, m_0)
    p_d = jnp.exp2(s_d - (m_init - c_d)).astype(BF16)
    p_0 = jnp.exp2(s_0 - (m_init - c_0)).astype(BF16)
    for k in range(nb):
        acc_sc[k] = (jnp.dot(p_d[rows_of(k)], vs_ref[pl.ds(q0a[k], Q_BLOCK), :], preferred_element_type=F32)
                     + jnp.dot(p_0[rows_of(k)], vs_ref[0:Q_BLOCK, :], preferred_element_type=F32))
        m_sc[k] = m_init[rows_of(k)]

    blk = lax.broadcasted_iota(jnp.int32, (n_sel, nb * Q_BLOCK), 0)
    lane_q = lax.broadcasted_iota(jnp.int32, (n_sel, nb * Q_BLOCK), 1)
    t_l = q0s[0] + lane_q
    cur = t_l >> 6
    causal_b = blk * SEL_BLOCK <= t_l
    forced = (blk == 0) | (blk == cur) | (blk == cur - 1)
    score = jnp.where(causal_b & jnp.logical_not(forced), imp_t, -1.0)
    lanes_of = lambda k: slice(k * Q_BLOCK, (k + 1) * Q_BLOCK)
    blk_f = lax.broadcasted_iota(jnp.int32, (n_sel, Q_BLOCK), 0).astype(F32)
    scores = [score[:, lanes_of(k)] for k in range(nb)]
    for _ in range(n_top - 3):
        for k in range(nb):
            top = jnp.max(scores[k], axis=0, keepdims=True)
            first_blk = jnp.min(jnp.where(scores[k] == top, blk_f, float(n_sel)), axis=0, keepdims=True)
            scores[k] = jnp.where(blk_f == first_blk, -3e38, scores[k])
    sel = jnp.where(forced | (jnp.concatenate(scores, axis=1) < -1e38), 1.0, 0.0)
    q0_l = t_l & ~(Q_BLOCK - 1)
    in_loop = (sel > 0.0) & (blk * SEL_BLOCK < q0_l) & (blk > 0)
    bias_t = jnp.where(in_loop, 0.0, NEG)
    used = jnp.where(in_loop, 1.0, 0.0)
    blocks_per_tile = SEL_TILE // SEL_BLOCK
    for t in range(used_ref.shape[0]):
        rows_t = slice(t * blocks_per_tile, (t + 1) * blocks_per_tile)
        bits = [jnp.max(used[rows_t, k * Q_BLOCK:(k + 1) * Q_BLOCK]).astype(jnp.int32) << k for k in range(nb)]
        used_ref[t] = functools.reduce(jnp.bitwise_or, bits)
    bias = bias_t.T.astype(BF16)
    for k in range(nb):
        bias_k = bias[k * Q_BLOCK:(k + 1) * Q_BLOCK]
        qa_sc[k] = jnp.concatenate([jnp.concatenate([bias_k] * A_REP, axis=0), qs[k]], axis=1)
    return o_c, o_w


def _nsa_body(n_top, q_ref, gl_ref, kc_ref, vc_ref, ks_ref, vs_ref, kw_ref, vw_ref, *rest):
    bw_refs = rest[:NSA_BLOCKS]
    (dmc_ref, tri_ref, slope_ref, ovt_ref, gsp_ref, o_ref,
     m_sc, acc_sc, qa_sc, used_ref) = rest[NSA_BLOCKS:]
    first = pl.program_id(2) * NSA_BLOCKS
    slope = slope_ref[...]
    qb = q_ref[...]
    qs = [jnp.concatenate([qb[k * Q_BLOCK:(k + 1) * Q_BLOCK, r * LANES:(r + 1) * LANES]
                           for r in range(A_REP)], axis=0) for k in range(NSA_BLOCKS)]
    o_c, o_w = _nsa_front(n_top, first, qs, kc_ref, vc_ref, ks_ref, vs_ref, kw_ref, vw_ref,
                          bw_refs, dmc_ref, tri_ref, slope, ovt_ref, m_sc, acc_sc, qa_sc, used_ref)

    def tile_update(tile, blocks):
        kb = pl.multiple_of(tile * SEL_TILE, SEL_TILE)
        scores = {k: _dot_nt(qa_sc[k], ks_ref[pl.ds(kb, SEL_TILE), :]) for k in blocks}
        for k in blocks:
            s = scores[k]
            c = slope * (kb - (first + k) * Q_BLOCK).astype(F32)
            mx = s[:, 0:LANES]
            for j in range(1, SEL_CHUNKS):
                mx = jnp.maximum(mx, s[:, j * LANES:(j + 1) * LANES])
            m_old = m_sc[k]
            m_new = jnp.maximum(m_old, jnp.max(mx, axis=1, keepdims=True) + c)
            shift = m_new - c
            p = jnp.concatenate([jnp.exp2(s[:, j * LANES:(j + 1) * LANES] - shift).astype(BF16)
                                 for j in range(SEL_CHUNKS)], axis=1)
            pv = jnp.dot(p, vs_ref[pl.ds(kb, SEL_TILE), :], preferred_element_type=F32)
            acc_sc[k] = jnp.exp2(m_old - m_new) * acc_sc[k] + pv
            m_sc[k] = m_new

    def sel_tile(tile, carry):
        code = used_ref[tile]
        for subset in range(1, 1 << NSA_BLOCKS):
            blocks = [k for k in range(NSA_BLOCKS) if subset >> k & 1]
            pl.when(code == subset)(functools.partial(tile_update, tile, blocks))
        return carry

    q0_last = (first + NSA_BLOCKS - 1) * Q_BLOCK
    lax.fori_loop(0, (q0_last + SEL_TILE - 1) // SEL_TILE, sel_tile, 0)

    gate = _sigmoid(gl_ref[...])
    g_hi = gate.astype(BF16)
    g_lo = (gate - g_hi.astype(F32)).astype(BF16)
    g_rep = jnp.dot(jnp.concatenate([g_hi, g_lo], axis=1), gsp_ref[...],
                    preferred_element_type=F32)
    for k in range(NSA_BLOCKS):
        acc = acc_sc[k]
        o_s = acc[:, :HEAD_DIM] / acc[:, HEAD_DIM:HEAD_DIM + 1]
        q_rows = slice(k * Q_BLOCK, (k + 1) * Q_BLOCK)
        g_slot = lambda n: g_rep[q_rows, n * LANES:n * LANES + HEAD_DIM]
        for r in range(A_REP):
            rs = slice(r * Q_BLOCK, (r + 1) * Q_BLOCK)
            rk = slice(k * QROWS + r * Q_BLOCK, k * QROWS + (r + 1) * Q_BLOCK)
            o_ref[q_rows, r * HEAD_DIM:(r + 1) * HEAD_DIM] = (
                g_slot(r) * o_c[rk] + g_slot(A_REP + r) * o_s[rs] + g_slot(2 * A_REP + r) * o_w[rk])


def _nsa(q_pad, glog, kc, vc, ks_ext, vs_ext, kw_ext, vw_ext, bw, dmc, tri, slope_b, ovt, gsp):
    b, g, s = ks_ext.shape[:3]
    nq = s // Q_BLOCK
    steps = nq // NSA_BLOCKS
    rows = NSA_BLOCKS * Q_BLOCK
    ncp = kc.shape[2]
    n_top = min(SEL_TOP_N, s // SEL_BLOCK)
    assert n_top >= 3
    n_win_cases = bw.shape[1]
    per_bg = lambda shp: pl.BlockSpec((None, None) + shp, lambda bi, gi, i: (bi, gi, 0, 0))
    per_g = lambda shp: pl.BlockSpec((None,) + shp, lambda bi, gi, i: (gi, 0, 0))
    const = lambda shp: pl.BlockSpec(shp, lambda bi, gi, i: (0, 0))
    win_case = lambda k: pl.BlockSpec(
        (None, None, QROWS, WIN_KEYS),
        lambda bi, gi, i: (gi, jnp.minimum(i * NSA_BLOCKS + k, n_win_cases - 1), 0, 0))
    return pl.pallas_call(
        functools.partial(_nsa_body, n_top),
        grid=(b, g, steps),
        in_specs=[pl.BlockSpec((rows, A_REP * LANES), lambda bi, gi, i: (bi * steps + i, gi)),
                  pl.BlockSpec((rows, LANES), lambda bi, gi, i: (bi * steps + i, gi)),
                  per_bg((ncp, LANES)), per_bg((ncp, LANES)),
                  per_bg((s, ks_ext.shape[3])), per_bg((s, LANES)),
                  per_bg((s, LANES)), per_bg((s, LANES))]
                 + [win_case(k) for k in range(NSA_BLOCKS)]
                 + [const((QROWS, ncp)), const((QROWS, Q_BLOCK)),
                    per_g((QROWS, LANES)), const(ovt.shape), const(gsp.shape)],
        out_specs=pl.BlockSpec((None, rows, A_REP * HEAD_DIM), lambda bi, gi, i: (bi, i, gi)),
        out_shape=jax.ShapeDtypeStruct((b, s, A_WIDTH), F32),
        scratch_shapes=[pltpu.VMEM((NSA_BLOCKS, QROWS, LANES), F32),
                        pltpu.VMEM((NSA_BLOCKS, QROWS, LANES), F32),
                        pltpu.VMEM((NSA_BLOCKS, QROWS, 2 * LANES), BF16),
                        pltpu.SMEM((s // SEL_TILE,), jnp.int32)],
        compiler_params=_cparams(3),
    )(q_pad, glog, kc, vc, ks_ext, vs_ext, kw_ext, vw_ext, *([bw] * NSA_BLOCKS),
      dmc, tri, slope_b, ovt, gsp)


def _gelu(x):
    return 0.5 * x * (1.0 + jnp.tanh(np.float32(np.sqrt(2.0 / np.pi)) * (x + 0.044715 * (x * x * x))))


def _sgu_body(uv_ref, zb_ref, lng_ref, lnb_ref, w_ref, bs_ref, o_ref):
    bw = zb_ref.shape[1]
    gd = bw // B_GROUPS
    u = _gelu(uv_ref[:, :bw])
    v = _gelu(uv_ref[:, bw:])
    mu = jnp.mean(v, axis=-1, keepdims=True)
    vc = v - mu
    vn = vc * lax.rsqrt(jnp.mean(vc * vc, axis=-1, keepdims=True) + EPS)
    vn = (vn * lng_ref[...] + lnb_ref[...]).astype(BF16)
    zb = zb_ref[...]
    gate = u * (zb * _sigmoid(zb))
    ti = lax.broadcasted_iota(jnp.int32, (B_CHUNK, B_CHUNK), 0)
    si = lax.broadcasted_iota(jnp.int32, (B_CHUNK, B_CHUNK), 1)
    for gi in range(B_GROUPS):
        wg = jnp.where(si <= ti, w_ref[gi], 0.0).astype(BF16)
        bcol = bs_ref[:, gi:gi + 1]
        for c in range(uv_ref.shape[0] // B_CHUNK):
            rs = slice(c * B_CHUNK, (c + 1) * B_CHUNK)
            cs = slice(gi * gd, (gi + 1) * gd)
            sv = jnp.dot(wg, vn[rs, cs], preferred_element_type=F32) + bcol
            o_ref[rs, cs] = (gate[rs, cs] * sv).astype(o_ref.dtype)


def _sgu(uv, zb, ln_g, ln_b, w_s, b_s):
    m, bw2 = uv.shape
    bw = bw2 // 2
    return pl.pallas_call(
        _sgu_body,
        grid=(m // ROW_TILE,),
        in_specs=[_row_tile_spec(bw2), _row_tile_spec(bw),
                  _resident_spec((1, bw)), _resident_spec((1, bw)),
                  _resident_spec(w_s.shape), _resident_spec((B_CHUNK, B_GROUPS))],
        out_specs=_row_tile_spec(bw),
        out_shape=jax.ShapeDtypeStruct((m, bw), BF16),
        compiler_params=_cparams(1),
    )(uv, zb, ln_g.reshape(1, bw), ln_b.reshape(1, bw), w_s, b_s.T)


def _merge_body(oa_ref, za_ref, ob_ref, ml_ref, wa_ref, wb_ref, o_ref):
    d = o_ref.shape[1]
    za = za_ref[...]
    oa = (oa_ref[...] * (za * _sigmoid(za))).astype(BF16)
    ua = jnp.dot(oa, wa_ref[...], preferred_element_type=F32)
    ub = jnp.dot(ob_ref[...], wb_ref[...], preferred_element_type=F32)
    o_ref[...] = (_sigmoid(ml_ref[:, :d]) * ua + _sigmoid(ml_ref[:, d:]) * ub).astype(o_ref.dtype)


def _merge(oa, za, ob, ml, w_up_a, w_up_b):
    m, aw = oa.shape
    bw = ob.shape[1]
    d = w_up_a.shape[1]
    tm = ROW_TILE
    row = lambda w: pl.BlockSpec((tm, w), lambda i: (i, 0))
    return pl.pallas_call(
        _merge_body,
        grid=(m // tm,),
        in_specs=[row(aw), row(aw), row(bw), row(2 * d),
                  _resident_spec((aw, d)), _resident_spec((bw, d))],
        out_specs=row(d),
        out_shape=jax.ShapeDtypeStruct((m, d), BF16),
        compiler_params=_cparams(1),
    )(oa, za, ob, ml, w_up_a, w_up_b)


def _out_body(x_ref, mg_ref, p_ref, wo_ref, wg_ref, wp_ref, fg_ref, o_ref):
    x1 = x_ref[...] + jnp.dot(mg_ref[...], wo_ref[...], preferred_element_type=F32)
    gl = jnp.dot(x1.astype(BF16), wg_ref[...], preferred_element_type=F32)
    ple = jnp.dot(p_ref[...].astype(BF16), wp_ref[...], preferred_element_type=F32)
    x2 = x1 + _sigmoid(gl) * ple
    y = x2 * lax.rsqrt(jnp.mean(x2 * x2, axis=-1, keepdims=True) + EPS)
    o_ref[...] = y * fg_ref[...]


def _out(x2d, merged, p2d, w_out, w_gate, w_ple, final_g):
    m, d = x2d.shape
    pd = p2d.shape[1]
    tm = ROW_TILE
    row = lambda w: pl.BlockSpec((tm, w), lambda i: (i, 0))
    return pl.pallas_call(
        _out_body,
        grid=(m // tm,),
        in_specs=[row(d), row(d), row(pd),
                  _resident_spec((d, d)), _resident_spec((d, d)), _resident_spec((pd, d)),
                  _resident_spec((1, d))],
        out_specs=row(d),
        out_shape=jax.ShapeDtypeStruct((m, d), F32),
        compiler_params=_cparams(1),
    )(x2d, merged, p2d, w_out, w_gate, w_ple, final_g.reshape(1, d))


def _bf16_parts(x, n):
    parts, rest = [], jnp.asarray(x, F32)
    for _ in range(n):
        piece = rest.astype(BF16)
        parts.append(piece)
        rest = rest - piece.astype(F32)
    return parts


def _attention_constants(s, ncp):
    hh = np.arange(1, A_HEADS + 1, dtype=np.float32)
    slopes = np.power(np.float32(2.0), -8.0 * hh / A_HEADS).astype(np.float32)
    slopes2 = (slopes * np.float32(LOG2E)).astype(np.float32)
    slope_col = jnp.asarray(np.repeat(slopes2.reshape(A_KV_GROUPS, A_REP), Q_BLOCK, axis=1)[:, :, None])
    slope_b = jnp.broadcast_to(slope_col, (A_KV_GROUPS, QROWS, LANES))
    qi = (jnp.arange(QROWS, dtype=jnp.int32) % Q_BLOCK).astype(F32)
    dm = jnp.arange(WIN_KEYS, dtype=F32)[None, :] - qi[:, None]
    offs = jnp.arange(WINDOW // Q_BLOCK + 1, dtype=F32)[:, None, None] * Q_BLOCK
    vis = (dm[None] <= offs) & (dm[None] > offs - WINDOW)
    bw = jnp.where(vis[None], (slope_col * dm[None])[:, None], NEG)
    tri = jnp.where(dm[:, :Q_BLOCK] <= 0, 0.0, NEG)
    c_end = jnp.arange(ncp, dtype=F32) * CMP_STRIDE + (CMP_LEN - 1)
    assert ((ncp - 1) * CMP_STRIDE + CMP_LEN - 1) >> POS_SHIFT <= 256, "ALiBi digits must be exact in bf16"
    dmc = c_end[None, :] - qi[:, None]
    ci = np.arange(ncp)[None, :]
    sj = np.arange(LANES)[:, None]
    ovt = ((CMP_STRIDE * ci < SEL_BLOCK * sj + SEL_BLOCK) &
           (CMP_STRIDE * ci + CMP_LEN > SEL_BLOCK * sj) &
           (ci < ncp - 1)).astype(np.float32)
    parts = _bf16_parts(slopes2, N_SLOPE_PARTS)
    cols = jnp.stack([pp * POS_LO for pp in parts] + parts, axis=1)
    q_cols = jnp.pad(cols, ((0, 0), (0, HEAD_DIM - cols.shape[1]))).reshape(1, A_HEADS * HEAD_DIM)
    n_gates = 3 * A_REP
    gsp = np.zeros((2 * LANES, n_gates * LANES), np.float32)
    for k in range(n_gates):
        gsp[k, k * LANES:k * LANES + HEAD_DIM] = 1.0
        gsp[LANES + k, k * LANES:k * LANES + HEAD_DIM] = 1.0
    return bw, dmc, tri, slope_b, jnp.asarray(ovt, dtype=BF16), q_cols, jnp.asarray(gsp, dtype=BF16)


def kernel(x, p, norm_g, w_in, cmp_pe_k, cmp_w1_k, cmp_w2_k, cmp_pe_v, cmp_w1_v, cmp_w2_v, ln_v_g, ln_v_b, sgu_w, sgu_b, w_up_a, w_up_b, w_out, w_ple, w_ple_gate, final_g):
    b, s, d = x.shape
    assert p.shape[0] == 1, "the final norm is fused into the (single) layer's output kernel"
    m = b * s
    g, r = A_KV_GROUPS, A_REP
    ncp = s // CMP_STRIDE
    b_width = w_up_b.shape[1]
    x2d = x.reshape(m, d)
    w_t = jnp.swapaxes(w_in[0], 0, 1)

    o_kv = A_WIDTH
    o_gl = o_kv + 6 * A_KV_WIDTH
    o_za = o_gl + 3 * A_HEADS
    o_uv = o_za + A_WIDTH
    o_zb = o_uv + 2 * b_width
    o_ml = o_zb + b_width
    wcast = lambda lo, hi: w_t[lo:hi].astype(BF16)
    w_q = (w_t[:o_kv] * np.float32(HEAD_DIM ** -0.5 * LOG2E)).astype(BF16)
    w_gl = w_t[o_gl:o_za].reshape(3, g, r, d).transpose(1, 0, 2, 3).reshape(g, 3 * r, d)
    w_gl = jnp.pad(w_gl, ((0, 0), (0, LANES - 3 * r), (0, 0))).reshape(g * LANES, d).astype(BF16)

    bw, dmc, tri, slope_b, ovt, q_cols, gsp = _attention_constants(s, ncp)

    h, q_pad = _norm_project_q(x2d, norm_g[0], w_q, q_cols)
    ck, cv = _project_kvc(h, wcast(o_kv, o_kv + 2 * A_KV_WIDTH), b, s)
    ks_ext, vs_ext, kw_ext, vw_ext = _project_kvsw(h, wcast(o_kv + 2 * A_KV_WIDTH, o_gl), b, s)
    glog = _project(h, w_gl, F32)
    za = _project(h, wcast(o_za, o_uv), F32)
    uv = _project(h, wcast(o_uv, o_zb), F32)
    zb = _project(h, wcast(o_zb, o_ml), F32)
    ml = _project(h, wcast(o_ml, w_t.shape[0]), F32)

    kc, vc = _compress(ck, cv, cmp_pe_k[0], cmp_w1_k[0], cmp_w2_k[0], cmp_pe_v[0], cmp_w1_v[0], cmp_w2_v[0])
    o_att = _nsa(q_pad, glog, kc, vc, ks_ext, vs_ext, kw_ext, vw_ext, bw, dmc, tri, slope_b, ovt, gsp)

    ob = _sgu(uv, zb, ln_v_g[0], ln_v_b[0], sgu_w[0], sgu_b[0])
    merged = _merge(o_att.reshape(m, A_WIDTH), za, ob, ml, w_up_a[0].astype(BF16), w_up_b[0].astype(BF16))
    out = _out(x2d, merged, p[0].reshape(m, -1), w_out[0].astype(BF16), w_ple_gate[0].astype(BF16),
               w_ple[0].astype(BF16), final_g)
    return out.reshape(b, s, d)
```

```python
import functools

import numpy as np
import jax
import jax.numpy as jnp
from jax import lax
from jax.experimental import pallas as pl
from jax.experimental.pallas import tpu as pltpu

A_HEADS = 16
A_KV_GROUPS = 4
A_REP = A_HEADS // A_KV_GROUPS
HEAD_DIM = 64
A_WIDTH = A_HEADS * HEAD_DIM
A_KV_WIDTH = A_KV_GROUPS * HEAD_DIM
CMP_LEN = 32
CMP_STRIDE = 16
SEL_BLOCK = 64
SEL_TOP_N = 16
WINDOW = 512
Q_BLOCK = 128
FORCE_BONUS = 1000.0
B_GROUPS = 8
B_CHUNK = 128
NEG = -1e30
EPS = 1e-6
LOG2E = float(np.log2(np.e))

LANES = 128
QROWS = A_REP * Q_BLOCK
NSA_BLOCKS = 2
SEL_TILE = 1024
SEL_CHUNKS = SEL_TILE // LANES
WIN_KEYS = WINDOW + Q_BLOCK
ROW_TILE = 512
PROJ_TILE = 1024
VMEM_LIMIT = 56 * 1024 * 1024
POS_SHIFT = 5
POS_LO = 1 << POS_SHIFT
N_SLOPE_PARTS = 3

F32 = jnp.float32
BF16 = jnp.bfloat16


def _cparams(n_axes):
    return pltpu.CompilerParams(dimension_semantics=("arbitrary",) * n_axes,
                                vmem_limit_bytes=VMEM_LIMIT)


def _sigmoid(x):
    return 1.0 / (1.0 + jnp.exp(-x))


def _dot_nt(a, b):
    return lax.dot_general(a, b, (((1,), (1,)), ((), ())), preferred_element_type=F32)


def _row_tile_spec(width):
    return pl.BlockSpec((ROW_TILE, width), lambda i: (i, 0))


def _resident_spec(shape):
    return pl.BlockSpec(shape, lambda i: (0,) * len(shape), pipeline_mode=pl.Buffered(1))


def _rmsnorm_rows(x, g):
    y = x * lax.rsqrt(jnp.mean(x * x, axis=-1, keepdims=True) + EPS)
    return (y * g).astype(BF16)


def _mm_body(h_ref, w_ref, o_ref):
    o_ref[...] = _dot_nt(h_ref[...], w_ref[...]).astype(o_ref.dtype)


def _project(h, w, out_dtype):
    m, k = h.shape
    n = w.shape[0]
    tn = min(n, 2048)
    return pl.pallas_call(
        _mm_body,
        grid=(n // tn, m // PROJ_TILE),
        in_specs=[pl.BlockSpec((PROJ_TILE, k), lambda j, i: (i, 0)),
                  pl.BlockSpec((tn, k), lambda j, i: (j, 0))],
        out_specs=pl.BlockSpec((PROJ_TILE, tn), lambda j, i: (i, j)),
        out_shape=jax.ShapeDtypeStruct((m, n), out_dtype),
        compiler_params=_cparams(2),
    )(h, w)


def _q_body(x_ref, g_ref, w_ref, qc_ref, h_ref, o_ref):
    h = _rmsnorm_rows(x_ref[...], g_ref[...])
    h_ref[...] = h
    res = _dot_nt(h, w_ref[...])
    rows = res.shape[0]
    for hd in range(A_HEADS):
        cs = slice(hd * HEAD_DIM, (hd + 1) * HEAD_DIM)
        cst = jnp.broadcast_to(qc_ref[:, cs], (rows, HEAD_DIM))
        o_ref[:, hd * LANES:(hd + 1) * LANES] = jnp.concatenate([res[:, cs].astype(BF16), cst], axis=1)


def _norm_project_q(x2d, norm_g, w_q, q_cols):
    m, k = x2d.shape
    return pl.pallas_call(
        _q_body,
        grid=(m // ROW_TILE,),
        in_specs=[_row_tile_spec(k), _resident_spec((1, k)), _resident_spec(w_q.shape),
                  _resident_spec(q_cols.shape)],
        out_specs=[_row_tile_spec(k), _row_tile_spec(A_HEADS * LANES)],
        out_shape=[jax.ShapeDtypeStruct((m, k), BF16), jax.ShapeDtypeStruct((m, A_HEADS * LANES), BF16)],
        compiler_params=_cparams(1),
    )(x2d, norm_g.reshape(1, k), w_q, q_cols)


def _kvsw_body(seq, h_ref, w_ref, ks_ref, vs_ref, kw_ref, vw_ref):
    res = _dot_nt(h_ref[...], w_ref[...])
    rows = res.shape[0]
    tok0 = (pl.program_id(0) * rows) % seq
    pos = tok0 + lax.broadcasted_iota(jnp.int32, (rows, LANES), 0)
    lane = lax.broadcasted_iota(jnp.int32, (rows, LANES), 1)
    onehot = jnp.where((pos >> 6) == lane, 1.0, 0.0).astype(BF16)
    off = (tok0 + lax.broadcasted_iota(jnp.int32, (rows, HEAD_DIM), 0)) & (SEL_TILE - 1)
    lane_h = lax.broadcasted_iota(jnp.int32, (rows, HEAD_DIM), 1)
    pos_cols = jnp.where(lane_h < N_SLOPE_PARTS, off >> POS_SHIFT,
                         jnp.where(lane_h < 2 * N_SLOPE_PARTS, off & (POS_LO - 1), 0)).astype(F32).astype(BF16)
    ones_col = jnp.where(lane_h == 0, 1.0, 0.0).astype(BF16)
    zeros = jnp.zeros((rows, HEAD_DIM), BF16)
    piece = lambda j: res[:, j * HEAD_DIM:(j + 1) * HEAD_DIM].astype(BF16)
    for g in range(A_KV_GROUPS):
        ks_ref[g, :, 0:LANES] = onehot
        ks_ref[g, :, LANES:2 * LANES] = jnp.concatenate([piece(g), pos_cols], axis=1)
        vs_ref[g] = jnp.concatenate([piece(A_KV_GROUPS + g), ones_col], axis=1)
        kw_ref[g] = jnp.concatenate([piece(2 * A_KV_GROUPS + g), zeros], axis=1)
        vw_ref[g] = jnp.concatenate([piece(3 * A_KV_GROUPS + g), ones_col], axis=1)


def _project_kvsw(h, w, b, s):
    m, k = h.shape
    tps = s // ROW_TILE
    assert s // SEL_BLOCK <= LANES, "one-hot selection columns must fit one lane tile"
    spec = lambda width: pl.BlockSpec((None, A_KV_GROUPS, ROW_TILE, width),
                                      lambda i: (i // tps, 0, i % tps, 0))
    shape = lambda width: jax.ShapeDtypeStruct((b, A_KV_GROUPS, s, width), BF16)
    return pl.pallas_call(
        lambda *refs: _kvsw_body(s, *refs),
        grid=(m // ROW_TILE,),
        in_specs=[_row_tile_spec(k), _resident_spec(w.shape)],
        out_specs=[spec(2 * LANES), spec(LANES), spec(LANES), spec(LANES)],
        out_shape=[shape(2 * LANES), shape(LANES), shape(LANES), shape(LANES)],
        compiler_params=_cparams(1),
    )(h, w)


def _kvc_body(h_ref, w_ref, ck_ref, cv_ref, res_sc):
    res = _dot_nt(h_ref[...], w_ref[...])
    n_slabs = res_sc.shape[0]
    for c in range(n_slabs):
        res_sc[c] = res[:, c * LANES:(c + 1) * LANES]
    nrow = res_sc.shape[1] // CMP_STRIDE
    heads_per_slab = LANES // HEAD_DIM
    for c in range(n_slabs):
        out_ref = ck_ref if c < n_slabs // 2 else cv_ref
        g0 = (c % (n_slabs // 2)) * heads_per_slab
        for l in range(CMP_STRIDE):
            slab = res_sc[c, pl.ds(l, nrow, stride=CMP_STRIDE), :]
            for j in range(heads_per_slab):
                out_ref[g0 + j, :, l * HEAD_DIM:(l + 1) * HEAD_DIM] = slab[:, j * HEAD_DIM:(j + 1) * HEAD_DIM]


def _project_kvc(h, w, b, s):
    m, k = h.shape
    tps = s // ROW_TILE
    nrow = ROW_TILE // CMP_STRIDE
    width = CMP_STRIDE * HEAD_DIM
    spec = pl.BlockSpec((None, A_KV_GROUPS, nrow, width), lambda i: (i // tps, 0, i % tps, 0))
    shape = jax.ShapeDtypeStruct((b, A_KV_GROUPS, s // CMP_STRIDE, width), F32)
    return pl.pallas_call(
        _kvc_body,
        grid=(m // ROW_TILE,),
        in_specs=[_row_tile_spec(k), _resident_spec(w.shape)],
        out_specs=[spec, spec],
        out_shape=[shape, shape],
        scratch_shapes=[pltpu.VMEM((w.shape[0] // LANES, ROW_TILE, LANES), F32)],
        compiler_params=_cparams(1),
    )(h, w)


def _dot_split(a, b):
    a_hi = a.astype(BF16)
    a_lo = (a - a_hi.astype(F32)).astype(BF16)
    b_hi = b.astype(BF16)
    b_lo = (b - b_hi.astype(F32)).astype(BF16)
    dot = lambda x, y: jnp.dot(x, y, preferred_element_type=F32)
    return dot(a_lo, b_hi) + dot(a_hi, b_lo) + dot(a_hi, b_hi)


def _compress_one(c, pe_ab, w1ab, w2p, ncp):
    h = _dot_split(c, w1ab) + jnp.sum(w1ab * pe_ab, axis=0, keepdims=True)
    pre = h[:, :HEAD_DIM] + pltpu.roll(h[:, HEAD_DIM:], ncp - 1, 0)
    hid = pre * _sigmoid(pre)
    out = _dot_split(hid, w2p)
    row = lax.broadcasted_iota(jnp.int32, out.shape, 0)
    return jnp.where(row < ncp - 1, out, 0.0)


def _compress_body(ck_ref, cv_ref, pek_ref, w1k_ref, w2k_ref, pev_ref, w1v_ref, w2v_ref,
                   kc_ref, vc_ref):
    ncp = ck_ref.shape[0]
    kc = _compress_one(ck_ref[...], pek_ref[...], w1k_ref[...], w2k_ref[...], ncp)
    vc = _compress_one(cv_ref[...], pev_ref[...], w1v_ref[...], w2v_ref[...], ncp)
    c_end = lax.broadcasted_iota(jnp.int32, kc.shape, 0) * CMP_STRIDE + (CMP_LEN - 1)
    col = lax.broadcasted_iota(jnp.int32, kc.shape, 1) - HEAD_DIM
    digits = jnp.where((col >= 0) & (col < N_SLOPE_PARTS), c_end >> POS_SHIFT,
                       jnp.where((col >= N_SLOPE_PARTS) & (col < 2 * N_SLOPE_PARTS), c_end & (POS_LO - 1), 0))
    kc = kc + digits.astype(F32)
    kc_ref[...] = kc.astype(kc_ref.dtype)
    vc_ref[...] = vc.astype(vc_ref.dtype)


def _compress(ck, cv, pe_k, w1_k, w2_k, pe_v, w1_v, w2_v):
    b, g, ncp, cw = ck.shape
    assert CMP_LEN == 2 * CMP_STRIDE and cw == CMP_STRIDE * HEAD_DIM
    pad2 = lambda w: jnp.pad(w, ((0, 0), (0, LANES - HEAD_DIM)))
    w1ab = lambda w1: jnp.concatenate(list(w1.reshape(2, cw, HEAD_DIM)), axis=1)
    pe_ab = lambda pe: jnp.repeat(pe.reshape(2, cw).T, HEAD_DIM, axis=1)
    tok = pl.BlockSpec((None, None, ncp, cw), lambda bi, gi: (bi, gi, 0, 0))
    full = lambda shp: pl.BlockSpec(shp, lambda bi, gi: (0,) * len(shp))
    out = pl.BlockSpec((None, None, ncp, LANES), lambda bi, gi: (bi, gi, 0, 0))
    return pl.pallas_call(
        _compress_body,
        grid=(b, g),
        in_specs=[tok, tok,
                  full((cw, LANES)), full((cw, LANES)), full((HEAD_DIM, LANES)),
                  full((cw, LANES)), full((cw, LANES)), full((HEAD_DIM, LANES))],
        out_specs=[out, out],
        out_shape=[jax.ShapeDtypeStruct((b, g, ncp, LANES), BF16)] * 2,
        compiler_params=_cparams(2),
    )(ck, cv, pe_ab(pe_k), w1ab(w1_k), pad2(w2_k), pe_ab(pe_v), w1ab(w1_v), pad2(w2_v))


def _nsa_front(n_top, first, qs, kc_ref, vc_ref, ks_ref, vs_ref, kw_ref, vw_ref,
               bw_refs, dmc_ref, tri_ref, slope, ovt_ref, m_sc, acc_sc, qa_sc, used_ref):
    nb = len(qs)
    q0s = [(first + k) * Q_BLOCK for k in range(nb)]
    n_sel = ovt_ref.shape[0]
    rows_of = lambda k: slice(k * QROWS, (k + 1) * QROWS)
    stack = lambda parts: jnp.concatenate(parts, axis=0)

    s_all = _dot_nt(stack(qs), kc_ref[...])
    dmc = dmc_ref[...]
    sc = stack([jnp.where(dmc <= q0s[k].astype(F32), s_all[rows_of(k)], NEG) for k in range(nb)])
    mc = jnp.max(sc, axis=1, keepdims=True)
    pc = jnp.exp2(sc - mc)
    lc = jnp.sum(pc, axis=1, keepdims=True)
    pc = pc * jnp.where(mc > 0.5 * NEG, 1.0 / lc, 0.0)
    o_c = jnp.dot(pc.astype(BF16), vc_ref[...], preferred_element_type=F32)[:, :HEAD_DIM]

    head_sum = lambda k: functools.reduce(
        jnp.add, [pc[k * QROWS + r * Q_BLOCK:k * QROWS + (r + 1) * Q_BLOCK] for r in range(A_REP)])
    psum = stack([head_sum(k) for k in range(nb)])
    p1 = psum.astype(BF16)
    r1 = psum - p1.astype(F32)
    p2 = r1.astype(BF16)
    p3 = (r1 - p2.astype(F32)).astype(BF16)
    ovt = ovt_ref[...]
    imp_t = _dot_nt(ovt, p3) + _dot_nt(ovt, p2) + _dot_nt(ovt, p1)

    kb_w = [pl.multiple_of(jnp.maximum(q0 - WINDOW, 0), Q_BLOCK) for q0 in q0s]
    sw = stack([_dot_nt(qs[k], kw_ref[pl.ds(kb_w[k], WIN_KEYS), :]) + bw_refs[k][...] for k in range(nb)])
    pw = jnp.exp2(sw - jnp.max(sw, axis=1, keepdims=True)).astype(BF16)
    accw = stack([jnp.dot(pw[rows_of(k)], vw_ref[pl.ds(kb_w[k], WIN_KEYS), :], preferred_element_type=F32)
                  for k in range(nb)])
    o_w = accw[:, :HEAD_DIM] / accw[:, HEAD_DIM:HEAD_DIM + 1]

    q0a = [pl.multiple_of(q0, Q_BLOCK) for q0 in q0s]
    tri = tri_ref[...]
    lane = lax.broadcasted_iota(jnp.int32, (QROWS, Q_BLOCK), 1)
    k0_cols = ks_ref[0:Q_BLOCK, LANES:2 * LANES]
    def always_bias(k):
        c_d = slope * ((q0s[k] // SEL_TILE) * SEL_TILE - q0s[k]).astype(F32)
        c_0 = slope * (-q0s[k]).astype(F32)
        return jnp.concatenate([tri + c_d, jnp.where(lane < jnp.minimum(q0s[k], SEL_BLOCK), c_0, NEG)], axis=1)

    s_a = stack([_dot_nt(qs[k], jnp.concatenate([ks_ref[pl.ds(q0a[k], Q_BLOCK), LANES:2 * LANES], k0_cols],
                                                axis=0)) + always_bias(k) for k in range(nb)])
    m_init = jnp.broadcast_to(jnp.max(s_a, axis=1, keepdims=True), (nb * QROWS, LANES))
    p_a = jnp.concatenate([jnp.exp2(s_a[:, :Q_BLOCK] - m_init), jnp.exp2(s_a[:, Q_BLOCK:] - m_init)],
                          axis=1).astype(BF16)
    for k in range(nb):
        v_a = jnp.concatenate([vs_ref[pl.ds(q0a[k], Q_BLOCK), :], vs_ref[0:Q_BLOCK, :]], axis=0)
        acc_sc[k] = jnp.dot(p_a[rows_of(k)], v_a, preferred_element_type=F32)
        m_sc[k] = m_init[rows_of(k)]

    blk = lax.broadcasted_iota(jnp.int32, (n_sel, nb * Q_BLOCK), 0)
    lane_q = lax.broadcasted_iota(jnp.int32, (n_sel, nb * Q_BLOCK), 1)
    t_l = q0s[0] + lane_q
    cur = t_l >> 6
    causal_b = blk * SEL_BLOCK <= t_l
    forced = (blk == 0) | (blk == cur) | (blk == cur - 1)
    score = jnp.where(causal_b & jnp.logical_not(forced), imp_t, -1.0)
    lanes_of = lambda k: slice(k * Q_BLOCK, (k + 1) * Q_BLOCK)
    blk_f = lax.broadcasted_iota(jnp.int32, (n_sel, Q_BLOCK), 0).astype(F32)
    scores = [score[:, lanes_of(k)] for k in range(nb)]
    for _ in range(n_top - 3):
        for k in range(nb):
            top = jnp.max(scores[k], axis=0, keepdims=True)
            first_blk = jnp.min(jnp.where(scores[k] == top, blk_f, float(n_sel)), axis=0, keepdims=True)
            scores[k] = jnp.where(blk_f == first_blk, -3e38, scores[k])
    sel = jnp.where(forced | (jnp.concatenate(scores, axis=1) < -1e38), 1.0, 0.0)
    q0_l = t_l & ~(Q_BLOCK - 1)
    in_loop = (sel > 0.0) & (blk * SEL_BLOCK < q0_l) & (blk > 0)
    bias_t = jnp.where(in_loop, 0.0, NEG)
    used = jnp.where(in_loop, 1.0, 0.0)
    blocks_per_tile = SEL_TILE // SEL_BLOCK
    for t in range(used_ref.shape[0]):
        rows_t = slice(t * blocks_per_tile, (t + 1) * blocks_per_tile)
        bits = [jnp.max(used[rows_t, k * Q_BLOCK:(k + 1) * Q_BLOCK]).astype(jnp.int32) << k for k in range(nb)]
        used_ref[t] = functools.reduce(jnp.bitwise_or, bits)
    bias = bias_t.T.astype(BF16)
    for k in range(nb):
        bias_k = bias[k * Q_BLOCK:(k + 1) * Q_BLOCK]
        qa_sc[k] = jnp.concatenate([jnp.concatenate([bias_k] * A_REP, axis=0), qs[k]], axis=1)
    return o_c, o_w


def _nsa_body(n_top, q_ref, gl_ref, kc_ref, vc_ref, ks_ref, vs_ref, kw_ref, vw_ref, *rest):
    bw_refs = rest[:NSA_BLOCKS]
    (dmc_ref, tri_ref, slope_ref, ovt_ref, gsp_ref, o_ref,
     m_sc, acc_sc, qa_sc, used_ref) = rest[NSA_BLOCKS:]
    first = pl.program_id(2) * NSA_BLOCKS
    slope = slope_ref[...]
    qb = q_ref[...]
    qs = [jnp.concatenate([qb[k * Q_BLOCK:(k + 1) * Q_BLOCK, r * LANES:(r + 1) * LANES]
                           for r in range(A_REP)], axis=0) for k in range(NSA_BLOCKS)]
    o_c, o_w = _nsa_front(n_top, first, qs, kc_ref, vc_ref, ks_ref, vs_ref, kw_ref, vw_ref,
                          bw_refs, dmc_ref, tri_ref, slope, ovt_ref, m_sc, acc_sc, qa_sc, used_ref)

    def tile_update(tile, blocks):
        kb = pl.multiple_of(tile * SEL_TILE, SEL_TILE)
        scores = {k: _dot_nt(qa_sc[k], ks_ref[pl.ds(kb, SEL_TILE), :]) for k in blocks}
        for k in blocks:
            s = scores[k]
            c = slope * (kb - (first + k) * Q_BLOCK).astype(F32)
            mx = s[:, 0:LANES]
            for j in range(1, SEL_CHUNKS):
                mx = jnp.maximum(mx, s[:, j * LANES:(j + 1) * LANES])
            m_old = m_sc[k]
            m_new = jnp.maximum(m_old, jnp.max(mx, axis=1, keepdims=True) + c)
            shift = m_new - c
            p = jnp.concatenate([jnp.exp2(s[:, j * LANES:(j + 1) * LANES] - shift).astype(BF16)
                                 for j in range(SEL_CHUNKS)], axis=1)
            pv = jnp.dot(p, vs_ref[pl.ds(kb, SEL_TILE), :], preferred_element_type=F32)
            acc_sc[k] = jnp.exp2(m_old - m_new) * acc_sc[k] + pv
            m_sc[k] = m_new

    def sel_tile(tile, carry):
        code = used_ref[tile]
        for subset in range(1, 1 << NSA_BLOCKS):
            blocks = [k for k in range(NSA_BLOCKS) if subset >> k & 1]
            pl.when(code == subset)(functools.partial(tile_update, tile, blocks))
        return carry

    q0_last = (first + NSA_BLOCKS - 1) * Q_BLOCK
    lax.fori_loop(0, (q0_last + SEL_TILE - 1) // SEL_TILE, sel_tile, 0)

    gate = _sigmoid(gl_ref[...])
    g_hi = gate.astype(BF16)
    g_lo = (gate - g_hi.astype(F32)).astype(BF16)
    g_rep = jnp.dot(jnp.concatenate([g_hi, g_lo], axis=1), gsp_ref[...],
                    preferred_element_type=F32)
    for k in range(NSA_BLOCKS):
        acc = acc_sc[k]
        o_s = acc[:, :HEAD_DIM] / acc[:, HEAD_DIM:HEAD_DIM + 1]
        q_rows = slice(k * Q_BLOCK, (k + 1) * Q_BLOCK)
        g_slot = lambda n: g_rep[q_rows, n * LANES:n * LANES + HEAD_DIM]
        for r in range(A_REP):
            rs = slice(r * Q_BLOCK, (r + 1) * Q_BLOCK)
            rk = slice(k * QROWS + r * Q_BLOCK, k * QROWS + (r + 1) * Q_BLOCK)
            o_ref[q_rows, r * HEAD_DIM:(r + 1) * HEAD_DIM] = (
                g_slot(r) * o_c[rk] + g_slot(A_REP + r) * o_s[rs] + g_slot(2 * A_REP + r) * o_w[rk])


def _nsa(q_pad, glog, kc, vc, ks_ext, vs_ext, kw_ext, vw_ext, bw, dmc, tri, slope_b, ovt, gsp):
    b, g, s = ks_ext.shape[:3]
    nq = s // Q_BLOCK
    steps = nq // NSA_BLOCKS
    rows = NSA_BLOCKS * Q_BLOCK
    ncp = kc.shape[2]
    n_top = min(SEL_TOP_N, s // SEL_BLOCK)
    assert n_top >= 3
    n_win_cases = bw.shape[1]
    per_bg = lambda shp: pl.BlockSpec((None, None) + shp, lambda bi, gi, i: (bi, gi, 0, 0))
    per_g = lambda shp: pl.BlockSpec((None,) + shp, lambda bi, gi, i: (gi, 0, 0))
    const = lambda shp: pl.BlockSpec(shp, lambda bi, gi, i: (0, 0))
    win_case = lambda k: pl.BlockSpec(
        (None, None, QROWS, WIN_KEYS),
        lambda bi, gi, i: (gi, jnp.minimum(i * NSA_BLOCKS + k, n_win_cases - 1), 0, 0))
    return pl.pallas_call(
        functools.partial(_nsa_body, n_top),
        grid=(b, g, steps),
        in_specs=[pl.BlockSpec((rows, A_REP * LANES), lambda bi, gi, i: (bi * steps + i, gi)),
                  pl.BlockSpec((rows, LANES), lambda bi, gi, i: (bi * steps + i, gi)),
                  per_bg((ncp, LANES)), per_bg((ncp, LANES)),
                  per_bg((s, ks_ext.shape[3])), per_bg((s, LANES)),
                  per_bg((s, LANES)), per_bg((s, LANES))]
                 + [win_case(k) for k in range(NSA_BLOCKS)]
                 + [const((QROWS, ncp)), const((QROWS, Q_BLOCK)),
                    per_g((QROWS, LANES)), const(ovt.shape), const(gsp.shape)],
        out_specs=pl.BlockSpec((None, rows, A_REP * HEAD_DIM), lambda bi, gi, i: (bi, i, gi)),
        out_shape=jax.ShapeDtypeStruct((b, s, A_WIDTH), F32),
        scratch_shapes=[pltpu.VMEM((NSA_BLOCKS, QROWS, LANES), F32),
                        pltpu.VMEM((NSA_BLOCKS, QROWS, LANES), F32),
                        pltpu.VMEM((NSA_BLOCKS, QROWS, 2 * LANES), BF16),
                        pltpu.SMEM((s // SEL_TILE,), jnp.int32)],
        compiler_params=_cparams(3),
    )(q_pad, glog, kc, vc, ks_ext, vs_ext, kw_ext, vw_ext, *([bw] * NSA_BLOCKS),
      dmc, tri, slope_b, ovt, gsp)


def _gelu(x):
    return 0.5 * x * (1.0 + jnp.tanh(np.float32(np.sqrt(2.0 / np.pi)) * (x + 0.044715 * (x * x * x))))


def _sgu_body(uv_ref, zb_ref, lng_ref, lnb_ref, w_ref, bs_ref, o_ref):
    bw = zb_ref.shape[1]
    gd = bw // B_GROUPS
    u = _gelu(uv_ref[:, :bw])
    v = _gelu(uv_ref[:, bw:])
    mu = jnp.mean(v, axis=-1, keepdims=True)
    vc = v - mu
    vn = vc * lax.rsqrt(jnp.mean(vc * vc, axis=-1, keepdims=True) + EPS)
    vn = (vn * lng_ref[...] + lnb_ref[...]).astype(BF16)
    zb = zb_ref[...]
    gate = u * (zb * _sigmoid(zb))
    ti = lax.broadcasted_iota(jnp.int32, (B_CHUNK, B_CHUNK), 0)
    si = lax.broadcasted_iota(jnp.int32, (B_CHUNK, B_CHUNK), 1)
    for gi in range(B_GROUPS):
        wg = jnp.where(si <= ti, w_ref[gi], 0.0).astype(BF16)
        bcol = bs_ref[:, gi:gi + 1]
        for c in range(uv_ref.shape[0] // B_CHUNK):
            rs = slice(c * B_CHUNK, (c + 1) * B_CHUNK)
            cs = slice(gi * gd, (gi + 1) * gd)
            sv = jnp.dot(wg, vn[rs, cs], preferred_element_type=F32) + bcol
            o_ref[rs, cs] = (gate[rs, cs] * sv).astype(o_ref.dtype)


def _sgu(uv, zb, ln_g, ln_b, w_s, b_s):
    m, bw2 = uv.shape
    bw = bw2 // 2
    return pl.pallas_call(
        _sgu_body,
        grid=(m // ROW_TILE,),
        in_specs=[_row_tile_spec(bw2), _row_tile_spec(bw),
                  _resident_spec((1, bw)), _resident_spec((1, bw)),
                  _resident_spec(w_s.shape), _resident_spec((B_CHUNK, B_GROUPS))],
        out_specs=_row_tile_spec(bw),
        out_shape=jax.ShapeDtypeStruct((m, bw), BF16),
        compiler_params=_cparams(1),
    )(uv, zb, ln_g.reshape(1, bw), ln_b.reshape(1, bw), w_s, b_s.T)


def _merge_body(oa_ref, za_ref, ob_ref, ml_ref, wa_ref, wb_ref, o_ref):
    d = o_ref.shape[1]
    za = za_ref[...]
    oa = (oa_ref[...] * (za * _sigmoid(za))).astype(BF16)
    ua = jnp.dot(oa, wa_ref[...], preferred_element_type=F32)
    ub = jnp.dot(ob_ref[...], wb_ref[...], preferred_element_type=F32)
    o_ref[...] = (_sigmoid(ml_ref[:, :d]) * ua + _sigmoid(ml_ref[:, d:]) * ub).astype(o_ref.dtype)


def _merge(oa, za, ob, ml, w_up_a, w_up_b):
    m, aw = oa.shape
    bw = ob.shape[1]
    d = w_up_a.shape[1]
    tm = ROW_TILE
    row = lambda w: pl.BlockSpec((tm, w), lambda i: (i, 0))
    return pl.pallas_call(
        _merge_body,
        grid=(m // tm,),
        in_specs=[row(aw), row(aw), row(bw), row(2 * d),
                  _resident_spec((aw, d)), _resident_spec((bw, d))],
        out_specs=row(d),
        out_shape=jax.ShapeDtypeStruct((m, d), BF16),
        compiler_params=_cparams(1),
    )(oa, za, ob, ml, w_up_a, w_up_b)


def _out_body(x_ref, mg_ref, p_ref, wo_ref, wg_ref, wp_ref, fg_ref, o_ref):
    x1 = x_ref[...] + jnp.dot(mg_ref[...], wo_ref[...], preferred_element_type=F32)
    gl = jnp.dot(x1.astype(BF16), wg_ref[...], preferred_element_type=F32)
    ple = jnp.dot(p_ref[...].astype(BF16), wp_ref[...], preferred_element_type=F32)
    x2 = x1 + _sigmoid(gl) * ple
    y = x2 * lax.rsqrt(jnp.mean(x2 * x2, axis=-1, keepdims=True) + EPS)
    o_ref[...] = y * fg_ref[...]


def _out(x2d, merged, p2d, w_out, w_gate, w_ple, final_g):
    m, d = x2d.shape
    pd = p2d.shape[1]
    tm = ROW_TILE
    row = lambda w: pl.BlockSpec((tm, w), lambda i: (i, 0))
    return pl.pallas_call(
        _out_body,
        grid=(m // tm,),
        in_specs=[row(d), row(d), row(pd),
                  _resident_spec((d, d)), _resident_spec((d, d)), _resident_spec((pd, d)),
                  _resident_spec((1, d))],
        out_specs=row(d),
        out_shape=jax.ShapeDtypeStruct((m, d), F32),
        compiler_params=_cparams(1),
    )(x2d, merged, p2d, w_out, w_gate, w_ple, final_g.reshape(1, d))


def _bf16_parts(x, n):
    parts, rest = [], jnp.asarray(x, F32)
    for _ in range(n):
        piece = rest.astype(BF16)
        parts.append(piece)
        rest = rest - piece.astype(F32)
    return parts


def _attention_constants(s, ncp):
    hh = np.arange(1, A_HEADS + 1, dtype=np.float32)
    slopes = np.power(np.float32(2.0), -8.0 * hh / A_HEADS).astype(np.float32)
    slopes2 = (slopes * np.float32(LOG2E)).astype(np.float32)
    slope_col = jnp.asarray(np.repeat(slopes2.reshape(A_KV_GROUPS, A_REP), Q_BLOCK, axis=1)[:, :, None])
    slope_b = jnp.broadcast_to(slope_col, (A_KV_GROUPS, QROWS, LANES))
    qi = (jnp.arange(QROWS, dtype=jnp.int32) % Q_BLOCK).astype(F32)
    dm = jnp.arange(WIN_KEYS, dtype=F32)[None, :] - qi[:, None]
    offs = jnp.arange(WINDOW // Q_BLOCK + 1, dtype=F32)[:, None, None] * Q_BLOCK
    vis = (dm[None] <= offs) & (dm[None] > offs - WINDOW)
    bw = jnp.where(vis[None], (slope_col * dm[None])[:, None], NEG)
    tri = jnp.where(dm[:, :Q_BLOCK] <= 0, 0.0, NEG)
    c_end = jnp.arange(ncp, dtype=F32) * CMP_STRIDE + (CMP_LEN - 1)
    assert ((ncp - 1) * CMP_STRIDE + CMP_LEN - 1) >> POS_SHIFT <= 256, "ALiBi digits must be exact in bf16"
    dmc = c_end[None, :] - qi[:, None]
    ci = np.arange(ncp)[None, :]
    sj = np.arange(LANES)[:, None]
    ovt = ((CMP_STRIDE * ci < SEL_BLOCK * sj + SEL_BLOCK) &
           (CMP_STRIDE * ci + CMP_LEN > SEL_BLOCK * sj) &
           (ci < ncp - 1)).astype(np.float32)
    parts = _bf16_parts(slopes2, N_SLOPE_PARTS)
    cols = jnp.stack([pp * POS_LO for pp in parts] + parts, axis=1)
    q_cols = jnp.pad(cols, ((0, 0), (0, HEAD_DIM - cols.shape[1]))).reshape(1, A_HEADS * HEAD_DIM)
    n_gates = 3 * A_REP
    gsp = np.zeros((2 * LANES, n_gates * LANES), np.float32)
    for k in range(n_gates):
        gsp[k, k * LANES:k * LANES + HEAD_DIM] = 1.0
        gsp[LANES + k, k * LANES:k * LANES + HEAD_DIM] = 1.0
    return bw, dmc, tri, slope_b, jnp.asarray(ovt, dtype=BF16), q_cols, jnp.asarray(gsp, dtype=BF16)


def kernel(x, p, norm_g, w_in, cmp_pe_k, cmp_w1_k, cmp_w2_k, cmp_pe_v, cmp_w1_v, cmp_w2_v, ln_v_g, ln_v_b, sgu_w, sgu_b, w_up_a, w_up_b, w_out, w_ple, w_ple_gate, final_g):
    b, s, d = x.shape
    assert p.shape[0] == 1, "the final norm is fused into the (single) layer's output kernel"
    m = b * s
    g, r = A_KV_GROUPS, A_REP
    ncp = s // CMP_STRIDE
    b_width = w_up_b.shape[1]
    x2d = x.reshape(m, d)
    w_t = jnp.swapaxes(w_in[0], 0, 1)

    o_kv = A_WIDTH
    o_gl = o_kv + 6 * A_KV_WIDTH
    o_za = o_gl + 3 * A_HEADS
    o_uv = o_za + A_WIDTH
    o_zb = o_uv + 2 * b_width
    o_ml = o_zb + b_width
    wcast = lambda lo, hi: w_t[lo:hi].astype(BF16)
    w_q = (w_t[:o_kv] * np.float32(HEAD_DIM ** -0.5 * LOG2E)).astype(BF16)
    w_gl = w_t[o_gl:o_za].reshape(3, g, r, d).transpose(1, 0, 2, 3).reshape(g, 3 * r, d)
    w_gl = jnp.pad(w_gl, ((0, 0), (0, LANES - 3 * r), (0, 0))).reshape(g * LANES, d).astype(BF16)

    bw, dmc, tri, slope_b, ovt, q_cols, gsp = _attention_constants(s, ncp)

    h, q_pad = _norm_project_q(x2d, norm_g[0], w_q, q_cols)
    ck, cv = _project_kvc(h, wcast(o_kv, o_kv + 2 * A_KV_WIDTH), b, s)
    ks_ext, vs_ext, kw_ext, vw_ext = _project_kvsw(h, wcast(o_kv + 2 * A_KV_WIDTH, o_gl), b, s)
    glog = _project(h, w_gl, F32)
    za = _project(h, wcast(o_za, o_uv), F32)
    uv = _project(h, wcast(o_uv, o_zb), F32)
    zb = _project(h, wcast(o_zb, o_ml), F32)
    ml = _project(h, wcast(o_ml, w_t.shape[0]), F32)

    kc, vc = _compress(ck, cv, cmp_pe_k[0], cmp_w1_k[0], cmp_w2_k[0], cmp_pe_v[0], cmp_w1_v[0], cmp_w2_v[0])
    o_att = _nsa(q_pad, glog, kc, vc, ks_ext, vs_ext, kw_ext, vw_ext, bw, dmc, tri, slope_b, ovt, gsp)

    ob = _sgu(uv, zb, ln_v_g[0], ln_v_b[0], sgu_w[0], sgu_b[0])
    merged = _merge(o_att.reshape(m, A_WIDTH), za, ob, ml, w_up_a[0].astype(BF16), w_up_b[0].astype(BF16))
    out = _out(x2d, merged, p[0].reshape(m, -1), w_out[0].astype(BF16), w_ple_gate[0].astype(BF16),
               w_ple[0].astype(BF16), final_g)
    return out.reshape(b, s, d)
```
